```python
import math
import jax, jax.numpy as jnp
from jax import lax
import numpy as np

D_MODEL = 1024
BATCH = 4
SEQ = 4096
DEPTH = 4
DEC_BATCH = 128
DEC_SEQ = 4
PAST_LEN = 2048
PAGE_SIZE = 128

HEAD_DIM = 64
A_HEADS = D_MODEL // 256
B_HEADS = D_MODEL // 256
C_HEADS = D_MODEL // 256
IDX_HEADS = D_MODEL // 256
IDX_DIM = 64
TOPK_MAX = 256
ROPE_THETA = 500000.0
ROPE_DIM = HEAD_DIM // 4
D_FF = ((8 * D_MODEL // 3 + 255) // 256) * 256
Q_BLOCK = 128
NORM_EPS = 1e-6
A_W = A_HEADS * HEAD_DIM
B_W = B_HEADS * HEAD_DIM
C_W = C_HEADS * 2 * HEAD_DIM
IN_WIDTHS = (A_W, A_W, A_W, A_HEADS, B_W, B_W, B_W, IDX_HEADS * IDX_DIM, IDX_DIM, IDX_HEADS, C_W, C_W, C_W)
IN_SPLIT_POINTS = tuple(sum(IN_WIDTHS[:i + 1]) for i in range(len(IN_WIDTHS) - 1))
D_IN = sum(IN_WIDTHS)

kernel_name = 'fox_dsa_diff_gated_macaron_step'


def rmsnorm(x, g):
    xf = x.astype(jnp.float32)
    y = xf * lax.rsqrt(jnp.mean(xf * xf, axis=-1, keepdims=True) + NORM_EPS)
    return (y * g.astype(jnp.float32)).astype(x.dtype)


def rope_partial(x, pos):
    half = ROPE_DIM // 2
    inv_freq = ROPE_THETA ** (-jnp.arange(half, dtype=jnp.float32) / half)
    ang = pos.astype(jnp.float32)[:, None] * inv_freq[None, :]
    cos = jnp.cos(ang)[:, None, :]
    sin = jnp.sin(ang)[:, None, :]
    xr = x[..., :ROPE_DIM].astype(jnp.float32)
    x1, x2 = xr[..., :half], xr[..., half:]
    rot = jnp.concatenate([x1 * cos - x2 * sin, x2 * cos + x1 * sin], axis=-1).astype(x.dtype)
    return jnp.concatenate([rot, x[..., ROPE_DIM:]], axis=-1)


def swiglu(x, w_gate, w_up, w_down):
    return (jax.nn.silu(x @ w_gate) * (x @ w_up)) @ w_down


def query_blocks(fn, *q_args):
    tq = q_args[0].shape[1]
    blk = min(Q_BLOCK, tq)
    nb = tq // blk
    split = lambda a: jnp.moveaxis(a.reshape(a.shape[0], nb, blk, *a.shape[2:]), 1, 0)
    out = lax.map(lambda args: fn(*args), tuple(split(a) for a in q_args))
    out = jnp.moveaxis(out, 0, 1)
    return out.reshape(out.shape[0], tq, *out.shape[3:])


def fox_attention(q, k, v, q_cum, k_cum, q_pos, k_pos):
    scale = HEAD_DIM ** -0.5
    k_cum_h = jnp.swapaxes(k_cum, 1, 2)
    def block(qb, qcb, qpb):
        s = jnp.einsum('bqhd,bkhd->bhqk', qb, k, preferred_element_type=jnp.float32) * scale
        s = s + jnp.swapaxes(qcb, 1, 2)[..., :, None] - k_cum_h[..., None, :]
        mask = k_pos[None, :] <= qpb[0][:, None]
        p = jax.nn.softmax(jnp.where(mask, s, -jnp.inf), axis=-1)
        return jnp.einsum('bhqk,bkhd->bqhd', p.astype(v.dtype), v)
    return query_blocks(block, q, q_cum, q_pos[None])


def dsa_attention(q, k, v, iq, ik, iw, q_pos, k_pos):
    tk = k.shape[1]
    topk = min(TOPK_MAX, tk // 4)
    scale = HEAD_DIM ** -0.5
    gather_rows = jax.vmap(lambda rows, idx: rows[idx])
    def block(qb, iqb, iwb, qpb):
        qp = qpb[0]
        sc = jnp.einsum('bqhd,bkd->bqhk', iqb, ik, preferred_element_type=jnp.float32)
        score = jnp.einsum('bqh,bqhk->bqk', iwb.astype(jnp.float32), jax.nn.relu(sc))
        mask = k_pos[None, :] <= qp[:, None]
        score = jnp.where(mask[None], score, -jnp.inf)
        _, sel = lax.top_k(score, topk)
        valid = sel <= qp[None, :, None]
        ks = gather_rows(k, sel)
        vs = gather_rows(v, sel)
        s = jnp.einsum('bqhd,bqjhd->bhqj', qb, ks, preferred_element_type=jnp.float32) * scale
        p = jax.nn.softmax(jnp.where(valid[:, None], s, -jnp.inf), axis=-1)
        return jnp.einsum('bhqj,bqjhd->bqhd', p.astype(v.dtype), vs)
    return query_blocks(block, q, iq, iw, q_pos[None])


def diff_attention(q, k, v, lam, q_pos, k_pos):
    scale = HEAD_DIM ** -0.5
    def block(qb, qpb):
        s = jnp.einsum('bqhcd,bkhcd->bhcqk', qb, k, preferred_element_type=jnp.float32) * scale
        mask = k_pos[None, :] <= qpb[0][:, None]
        p = jax.nn.softmax(jnp.where(mask, s, -jnp.inf), axis=-1)
        a = p[:, :, 0] - lam * p[:, :, 1]
        return jnp.einsum('bhqk,bkhe->bqhe', a.astype(v.dtype), v)
    return query_blocks(block, q, q_pos[None])


def token_mixing(h, p, layer, past):
    bsz, t, _ = h.shape
    proj = jnp.einsum('btd,de->bte', h, p['w_in'])
    aq, ak, av, af, bq, bk, bv, biq, bik, biw, cq, ck, cv = jnp.split(proj, IN_SPLIT_POINTS, axis=-1)
    past_len = 0 if past is None else past[0].shape[1]
    tk = past_len + t
    k_pos = jnp.arange(tk, dtype=jnp.int32)
    q_pos = k_pos[past_len:]
    heads = lambda z, n, d: z.reshape(bsz, t, n, d)
    aq = rmsnorm(heads(aq, A_HEADS, HEAD_DIM), p['a_q_norm'])
    ak = rmsnorm(heads(ak, A_HEADS, HEAD_DIM), p['a_k_norm'])
    av = heads(av, A_HEADS, HEAD_DIM)
    a_logf = jax.nn.log_sigmoid(af.astype(jnp.float32) + p['fox_bias_f'].astype(jnp.float32))
    bq = rope_partial(rmsnorm(heads(bq, B_HEADS, HEAD_DIM), p['b_q_norm']), q_pos)
    bk = rope_partial(rmsnorm(heads(bk, B_HEADS, HEAD_DIM), p['b_k_norm']), q_pos)
    bv = heads(bv, B_HEADS, HEAD_DIM)
    biq = rope_partial(heads(biq, IDX_HEADS, IDX_DIM), q_pos)
    bik = rope_partial(bik[:, :, None, :], q_pos)[:, :, 0, :]
    cq = rmsnorm(cq.reshape(bsz, t, C_HEADS, 2, HEAD_DIM), p['c_q_norm'])
    cq = rope_partial(cq.reshape(bsz, t, 2 * C_HEADS, HEAD_DIM), q_pos).reshape(bsz, t, C_HEADS, 2, HEAD_DIM)
    ck = rmsnorm(ck.reshape(bsz, t, C_HEADS, 2, HEAD_DIM), p['c_k_norm'])
    ck = rope_partial(ck.reshape(bsz, t, 2 * C_HEADS, HEAD_DIM), q_pos).reshape(bsz, t, C_HEADS, 2, HEAD_DIM)
    cv = heads(cv, C_HEADS, 2 * HEAD_DIM)

    new_rows = (ak, av, a_logf, bk, bv, bik, ck, cv)
    if past is None:
        rows = new_rows
    else:
        rows = tuple(jnp.concatenate([old, new], axis=1) for old, new in zip(past, new_rows))
    k_a, v_a, logf_a, k_b, v_b, idx_b, k_c, v_c = rows

    cum = jnp.cumsum(logf_a.astype(jnp.float32), axis=1)
    out_a = fox_attention(aq, k_a, v_a, cum[:, past_len:], cum, q_pos, k_pos)
    out_b = dsa_attention(bq, k_b, v_b, biq, idx_b, biw, q_pos, k_pos)
    lam_init = 0.8 - 0.6 * math.exp(-0.3 * layer)
    f32 = jnp.float32
    lam = (jnp.exp(jnp.sum(p['c_lambda_q1'].astype(f32) * p['c_lambda_k1'].astype(f32)))
           - jnp.exp(jnp.sum(p['c_lambda_q2'].astype(f32) * p['c_lambda_k2'].astype(f32))) + lam_init)
    out_c = diff_attention(cq, k_c, v_c, lam, q_pos, k_pos)
    out_c = rmsnorm(out_c, p['c_subln']) * (1.0 - lam_init)

    br_a = out_a.reshape(bsz, t, A_W) @ p['w_o_a']
    br_b = out_b.reshape(bsz, t, B_W) @ p['w_o_b']
    br_c = out_c.reshape(bsz, t, C_W) @ p['w_o_c']
    gates = jax.nn.sigmoid((h @ p['w_gate'] + p['b_gate']).astype(f32)).astype(h.dtype)
    g_a, g_b, g_c = jnp.split(gates, 3, axis=-1)
    merged = g_a * br_a + g_b * br_b + g_c * br_c
    return merged @ p['w_out'], new_rows


def trunk_layer(x, p, layer, past):
    x = x + 0.5 * swiglu(rmsnorm(x, p['norm_ffn1']), p['ffn1_w_gate'], p['ffn1_w_up'], p['ffn1_w_down'])
    mixed, new_rows = token_mixing(rmsnorm(x, p['norm_mix']), p, layer, past)
    x = x + mixed
    x = x + 0.5 * swiglu(rmsnorm(x, p['norm_ffn2']), p['ffn2_w_gate'], p['ffn2_w_up'], p['ffn2_w_down'])
    return x, new_rows


def setup_inputs(seed: int = 0) -> dict:
    key = jax.random.key(seed)
    keys = jax.random.split(key, 48)
    counter = iter(range(48))
    nk = lambda: keys[next(counter)]
    f32 = jnp.float32
    def nrm(shape, scale=1.0):
        return jax.random.normal(nk(), shape, f32) * scale
    def gain(shape):
        return 1.0 + nrm(shape, 0.02)
    n_pages = PAST_LEN // PAGE_SIZE
    n_used = DEC_BATCH * n_pages
    n_pool = n_used + max(1, n_used // 4)
    pool = (DEPTH, n_pool, PAGE_SIZE)
    x_prompt = nrm((BATCH, SEQ, D_MODEL))
    x_sample = nrm((DEC_BATCH, DEC_SEQ, D_MODEL))
    cache_a_k = nrm(pool + (A_HEADS, HEAD_DIM))
    cache_a_v = nrm(pool + (A_HEADS, HEAD_DIM))
    cache_a_logf = jax.nn.log_sigmoid(2.5 + nrm(pool + (A_HEADS,)))
    cache_b_k = nrm(pool + (B_HEADS, HEAD_DIM))
    cache_b_v = nrm(pool + (B_HEADS, HEAD_DIM))
    cache_b_idx = nrm(pool + (IDX_DIM,))
    cache_c_k = nrm(pool + (C_HEADS, 2, HEAD_DIM))
    cache_c_v = nrm(pool + (C_HEADS, 2 * HEAD_DIM))
    page_table = jax.random.permutation(nk(), n_pool)[:n_used].reshape(DEC_BATCH, n_pages).astype(jnp.int32)
    return {
        'x_prompt': x_prompt, 'x_sample': x_sample,
        'cache_a_k': cache_a_k, 'cache_a_v': cache_a_v, 'cache_a_logf': cache_a_logf,
        'cache_b_k': cache_b_k, 'cache_b_v': cache_b_v, 'cache_b_idx': cache_b_idx,
        'cache_c_k': cache_c_k, 'cache_c_v': cache_c_v,
        'page_table': page_table,
        'norm_ffn1': gain((DEPTH, D_MODEL)),
        'ffn1_w_gate': nrm((DEPTH, D_MODEL, D_FF), D_MODEL ** -0.5),
        'ffn1_w_up': nrm((DEPTH, D_MODEL, D_FF), D_MODEL ** -0.5),
        'ffn1_w_down': nrm((DEPTH, D_FF, D_MODEL), D_FF ** -0.5),
        'norm_mix': gain((DEPTH, D_MODEL)),
        'w_in': nrm((DEPTH, D_MODEL, D_IN), D_MODEL ** -0.5),
        'fox_bias_f': jax.random.uniform(nk(), (DEPTH, A_HEADS), f32, 1.0, 4.0),
        'a_q_norm': gain((DEPTH, HEAD_DIM)), 'a_k_norm': gain((DEPTH, HEAD_DIM)),
        'b_q_norm': gain((DEPTH, HEAD_DIM)), 'b_k_norm': gain((DEPTH, HEAD_DIM)),
        'c_q_norm': gain((DEPTH, 2, HEAD_DIM)), 'c_k_norm': gain((DEPTH, 2, HEAD_DIM)),
        'c_lambda_q1': nrm((DEPTH, HEAD_DIM), 0.1), 'c_lambda_k1': nrm((DEPTH, HEAD_DIM), 0.1),
        'c_lambda_q2': nrm((DEPTH, HEAD_DIM), 0.1), 'c_lambda_k2': nrm((DEPTH, HEAD_DIM), 0.1),
        'c_subln': gain((DEPTH, 2 * HEAD_DIM)),
        'w_o_a': nrm((DEPTH, A_W, D_MODEL), A_W ** -0.5),
        'w_o_b': nrm((DEPTH, B_W, D_MODEL), B_W ** -0.5),
        'w_o_c': nrm((DEPTH, C_W, D_MODEL), C_W ** -0.5),
        'w_gate': nrm((DEPTH, D_MODEL, 3 * D_MODEL), D_MODEL ** -0.5),
        'b_gate': nrm((DEPTH, 3 * D_MODEL), 0.01),
        'w_out': nrm((DEPTH, D_MODEL, D_MODEL), D_MODEL ** -0.5),
        'norm_ffn2': gain((DEPTH, D_MODEL)),
        'ffn2_w_gate': nrm((DEPTH, D_MODEL, D_FF), D_MODEL ** -0.5),
        'ffn2_w_up': nrm((DEPTH, D_MODEL, D_FF), D_MODEL ** -0.5),
        'ffn2_w_down': nrm((DEPTH, D_FF, D_MODEL), D_FF ** -0.5),
    }


def reference(x_prompt, x_sample, cache_a_k, cache_a_v, cache_a_logf, cache_b_k, cache_b_v, cache_b_idx,
              cache_c_k, cache_c_v, page_table, norm_ffn1, ffn1_w_gate, ffn1_w_up, ffn1_w_down, norm_mix,
              w_in, fox_bias_f, a_q_norm, a_k_norm, b_q_norm, b_k_norm, c_q_norm, c_k_norm,
              c_lambda_q1, c_lambda_k1, c_lambda_q2, c_lambda_k2, c_subln, w_o_a, w_o_b, w_o_c,
              w_gate, b_gate, w_out, norm_ffn2, ffn2_w_gate, ffn2_w_up, ffn2_w_down):
    n_pages = PAST_LEN // PAGE_SIZE
    dec_batch = page_table.shape[0]
    caches = (cache_a_k, cache_a_v, cache_a_logf, cache_b_k, cache_b_v, cache_b_idx, cache_c_k, cache_c_v)
    y_prompt, y_sample = x_prompt, x_sample
    rows_p, rows_s = [], []
    for l in range(DEPTH):
        p = {
            'norm_ffn1': norm_ffn1[l], 'ffn1_w_gate': ffn1_w_gate[l], 'ffn1_w_up': ffn1_w_up[l],
            'ffn1_w_down': ffn1_w_down[l], 'norm_mix': norm_mix[l], 'w_in': w_in[l], 'fox_bias_f': fox_bias_f[l],
            'a_q_norm': a_q_norm[l], 'a_k_norm': a_k_norm[l], 'b_q_norm': b_q_norm[l], 'b_k_norm': b_k_norm[l],
            'c_q_norm': c_q_norm[l], 'c_k_norm': c_k_norm[l], 'c_lambda_q1': c_lambda_q1[l],
            'c_lambda_k1': c_lambda_k1[l], 'c_lambda_q2': c_lambda_q2[l], 'c_lambda_k2': c_lambda_k2[l],
            'c_subln': c_subln[l], 'w_o_a': w_o_a[l], 'w_o_b': w_o_b[l], 'w_o_c': w_o_c[l],
            'w_gate': w_gate[l], 'b_gate': b_gate[l], 'w_out': w_out[l], 'norm_ffn2': norm_ffn2[l],
            'ffn2_w_gate': ffn2_w_gate[l], 'ffn2_w_up': ffn2_w_up[l], 'ffn2_w_down': ffn2_w_down[l],
        }
        past = tuple(c[l, page_table].reshape(dec_batch, n_pages * PAGE_SIZE, *c.shape[3:]) for c in caches)
        y_prompt, new_p = trunk_layer(y_prompt, p, l, None)
        y_sample, new_s = trunk_layer(y_sample, p, l, past)
        rows_p.append(new_p)
        rows_s.append(new_s)
    a_k_p, a_v_p, a_f_p, b_k_p, b_v_p, b_i_p, c_k_p, c_v_p = [jnp.stack(z) for z in zip(*rows_p)]
    a_k_s, a_v_s, a_f_s, b_k_s, b_v_s, b_i_s, c_k_s, c_v_s = [jnp.stack(z) for z in zip(*rows_s)]
    return (y_prompt, y_sample, a_k_p, a_k_s, a_v_p, a_v_s, a_f_p, a_f_s, b_k_p, b_k_s, b_v_p, b_v_s,
            b_i_p, b_i_s, c_k_p, c_k_s, c_v_p, c_v_s)
```

```python
import functools
import math

import jax
import jax.numpy as jnp
from jax import lax
from jax.experimental import pallas as pl
from jax.experimental.pallas import tpu as pltpu

F32 = jnp.float32
BF16 = jnp.bfloat16

HEAD_DIM = 64
ROPE_DIM = HEAD_DIM // 4
ROPE_THETA = 500000.0
NORM_EPS = 1e-6
TOPK_MAX = 256
PAGE_SIZE = 128
N_HEADS = 4
HW = N_HEADS * HEAD_DIM
CW = 2 * HW
MISC_W = 128
NEG = -1e30
SUB = 8
VMEM_LIMIT = 56 * 1024 * 1024


def _cparams(*sem):
    return pltpu.CompilerParams(dimension_semantics=sem, vmem_limit_bytes=VMEM_LIMIT)


def _const_spec(shape):
    nd = len(shape)
    return pl.BlockSpec(shape, lambda *_: (0,) * nd, pipeline_mode=pl.Buffered(1))


def _rms(x, g):
    ms = jnp.mean(x * x, axis=-1, keepdims=True)
    return x * lax.rsqrt(ms + NORM_EPS) * g


def _sigmoid(x):
    return 1.0 / (1.0 + jnp.exp(-x))


def _lane_head(shape, width=HEAD_DIM):
    return lax.broadcasted_iota(jnp.int32, shape, len(shape) - 1) // width


def _ffn_body(x_ref, g_ref, wg_ref, wu_ref, wd_ref, o_ref, acc_ref, *, fc):
    x = x_ref[...]
    h = _rms(x, g_ref[...]).astype(BF16)
    dff = wg_ref.shape[1]
    for c in range(dff // fc):
        sl = slice(c * fc, (c + 1) * fc)
        g = jnp.dot(h, wg_ref[:, sl], preferred_element_type=F32)
        u = jnp.dot(h, wu_ref[:, sl], preferred_element_type=F32)
        a = (g * _sigmoid(g) * u).astype(BF16)
        d = jnp.dot(a, wd_ref[sl, :], preferred_element_type=F32)
        if c == 0:
            acc_ref[...] = d
        else:
            acc_ref[...] += d
    o_ref[...] = x + 0.5 * acc_ref[...]


def _ffn(x, g, wg, wu, wd, tm):
    n, d = x.shape
    dff = wg.shape[1]
    return pl.pallas_call(
        functools.partial(_ffn_body, fc=256),
        grid=(n // tm,),
        in_specs=[pl.BlockSpec((tm, d), lambda i: (i, 0)),
                  _const_spec((1, d)), _const_spec((d, dff)), _const_spec((d, dff)), _const_spec((dff, d))],
        out_specs=pl.BlockSpec((tm, d), lambda i: (i, 0)),
        out_shape=jax.ShapeDtypeStruct((n, d), F32),
        scratch_shapes=[pltpu.VMEM((tm, d), F32)],
        compiler_params=_cparams("arbitrary"),
        name="ffn",
    )(x, g, wg, wu, wd)


_O_AQ, _O_AK, _O_AV = 0, HW, 2 * HW
_O_BQ, _O_BK, _O_BV, _O_BIQ, _O_BIK = 3 * HW, 4 * HW, 5 * HW, 6 * HW, 7 * HW
_O_CQ, _O_CK, _O_CV = 8 * HW, 8 * HW + CW, 8 * HW + 2 * CW
_O_MISC = 8 * HW + 3 * CW
PACK_W = _O_MISC + MISC_W


def _pack_w_in(w):
    widths = (HW, HW, HW, N_HEADS, HW, HW, HW, HW, HEAD_DIM, N_HEADS, CW, CW, CW)
    offs = [0]
    for wd in widths:
        offs.append(offs[-1] + wd)
    part = lambda i: w[:, offs[i]:offs[i + 1]]
    aq, ak, av, af, bq, bk, bv, biq, bik, biw, cq, ck, cv = [part(i) for i in range(13)]
    misc = jnp.concatenate([af, biw, jnp.zeros((w.shape[0], MISC_W - 2 * N_HEADS), w.dtype)], axis=1)
    return jnp.concatenate([aq, ak, av, bq, bk, bv, biq, jnp.tile(bik, (1, N_HEADS)), cq, ck, cv, misc], axis=1)


def _headnorm(x, bd_ref, g):
    sq = x * x
    hi = sq.astype(BF16)
    lo = (sq - hi.astype(F32)).astype(BF16)
    ms = (jnp.dot(hi, bd_ref[...], preferred_element_type=F32)
          + jnp.dot(lo, bd_ref[...], preferred_element_type=F32))
    return x * lax.rsqrt(ms + NORM_EPS) * g


def _rope(x, cos, sa, sb):
    w = x.shape[1]
    rep = w // cos.shape[1]
    if rep > 1:
        cos, sa, sb = (jnp.concatenate([t] * rep, axis=1) for t in (cos, sa, sb))
    half = ROPE_DIM // 2
    return x * cos + pltpu.roll(x, half, axis=1) * sa + pltpu.roll(x, w - half, axis=1) * sb


def _inproj_body(x_ref, g_ref, w_ref, fb_ref, gaq_ref, gak_ref, gbq_ref, gbk_ref, gcq_ref, gck_ref,
                 cos_ref, sa_ref, sb_ref, bd_ref,
                 aq_o, akf_o, akb_o, avf_o, avb_o, bq_o, bkf_o, bkb_o, bvf_o, bvb_o, biq_o, bif_o, bib_o,
                 cq_o, ckf_o, ckb_o, cvf_o, cvb_o, misc_o, cum_o, carry_ref, *, tiles_per_seq, scale):
    i = pl.program_id(0)
    x = x_ref[...]
    h = _rms(x, g_ref[...]).astype(BF16)
    proj = lambda off, wd: jnp.dot(h, w_ref[:, off:off + wd], preferred_element_type=F32)
    cos, sa, sb = cos_ref[...], sa_ref[...], sb_ref[...]

    aq = _headnorm(proj(_O_AQ, HW), bd_ref, gaq_ref[...])
    aq_o[...] = (aq * scale).astype(BF16)
    ak = _headnorm(proj(_O_AK, HW), bd_ref, gak_ref[...])
    akf_o[...] = ak
    akb_o[...] = ak.astype(BF16)
    av = proj(_O_AV, HW)
    avf_o[...] = av
    avb_o[...] = av.astype(BF16)

    bq = _rope(_headnorm(proj(_O_BQ, HW), bd_ref, gbq_ref[...]), cos, sa, sb)
    bq_o[...] = (bq * scale).astype(BF16)
    bk = _rope(_headnorm(proj(_O_BK, HW), bd_ref, gbk_ref[...]), cos, sa, sb)
    bkf_o[...] = bk
    bkb_o[...] = bk.astype(BF16)
    bv = proj(_O_BV, HW)
    bvf_o[...] = bv
    bvb_o[...] = bv.astype(BF16)
    biq_o[...] = _rope(proj(_O_BIQ, HW), cos, sa, sb).astype(BF16)
    bik = _rope(proj(_O_BIK, HW), cos, sa, sb)
    bif_o[...] = bik[:, :HEAD_DIM]
    bib_o[...] = bik.astype(BF16)

    for half in range(2):
        sl = slice(half * HW, (half + 1) * HW)
        cq = _rope(_headnorm(proj(_O_CQ + half * HW, HW), bd_ref, gcq_ref[:, sl]), cos, sa, sb)
        cq_o[:, sl] = (cq * scale).astype(BF16)
        ck = _rope(_headnorm(proj(_O_CK + half * HW, HW), bd_ref, gck_ref[:, sl]), cos, sa, sb)
        ckf_o[:, sl] = ck
        ckb_o[:, sl] = ck.astype(BF16)
    cv = proj(_O_CV, CW)
    cvf_o[...] = cv
    cvb_o[...] = cv.astype(BF16)

    misc = proj(_O_MISC, MISC_W)
    z = misc + fb_ref[...]
    logf = jnp.minimum(z, 0.0) - jnp.log(1.0 + jnp.exp(-jnp.abs(z)))
    lane = lax.broadcasted_iota(jnp.int32, misc.shape, 1)
    misc = jnp.where(lane < N_HEADS, logf, misc)
    misc_o[...] = misc

    tm = misc.shape[0]
    r = lax.broadcasted_iota(jnp.int32, (tm, tm), 0)
    c = lax.broadcasted_iota(jnp.int32, (tm, tm), 1)
    tri = jnp.where(c <= r, 1.0, 0.0).astype(F32)

    @pl.when(i % tiles_per_seq == 0)
    def _():
        carry_ref[...] = jnp.zeros_like(carry_ref)

    cum = jnp.dot(tri, misc, preferred_element_type=F32, precision=lax.Precision.HIGHEST) + carry_ref[...]
    cum_o[...] = cum
    carry_ref[...] = cum[tm - 1:tm, :]


def _inproj(x, p, tabs, tm, seq_len):
    n, d = x.shape
    cos, sa, sb = tabs
    tab_tiles = cos.shape[0] // tm
    row = lambda w: pl.BlockSpec((tm, w), lambda i: (i, 0))
    tab = pl.BlockSpec((tm, cos.shape[1]), lambda i: (i % tab_tiles, 0))
    sds = lambda w, dt: jax.ShapeDtypeStruct((n, w), dt)
    outs = [
        (HW, BF16),
        (HW, F32), (HW, BF16), (HW, F32), (HW, BF16),
        (HW, BF16),
        (HW, F32), (HW, BF16), (HW, F32), (HW, BF16),
        (HW, BF16),
        (HEAD_DIM, F32), (HW, BF16),
        (CW, BF16),
        (CW, F32), (CW, BF16), (CW, F32), (CW, BF16),
        (MISC_W, F32), (MISC_W, F32),
    ]
    return pl.pallas_call(
        functools.partial(_inproj_body, tiles_per_seq=max(seq_len // tm, 1), scale=HEAD_DIM ** -0.5),
        grid=(n // tm,),
        in_specs=[row(d), _const_spec((1, d)), _const_spec((d, PACK_W)), _const_spec((1, MISC_W)),
                  _const_spec((1, HW)), _const_spec((1, HW)), _const_spec((1, HW)), _const_spec((1, HW)),
                  _const_spec((1, CW)), _const_spec((1, CW)), tab, tab, tab, _const_spec((HW, HW))],
        out_specs=[row(w) for w, _ in outs],
        out_shape=[sds(w, dt) for w, dt in outs],
        scratch_shapes=[pltpu.VMEM((1, MISC_W), F32)],
        compiler_params=_cparams("arbitrary"),
        name="inproj",
    )(x, p["norm_mix"], p["w_in"], p["fox_bias"], p["gaq"], p["gak"], p["gbq"], p["gbk"], p["gcq"], p["gck"],
      cos, sa, sb, p["bd"])


def _online_update(s, m_prev, l_prev):
    m_new = jnp.maximum(m_prev, jnp.max(s, axis=-1, keepdims=True))
    alpha = jnp.exp(m_prev - m_new)
    p = jnp.exp(s - m_new)
    l_new = alpha * l_prev + jnp.sum(p, axis=-1, keepdims=True)
    return p, m_new, l_new, alpha


def _qk(q, k):
    return lax.dot_general(q, k, (((1,), (1,)), ((), ())), preferred_element_type=F32)


def _fox_body(q_ref, k_ref, v_ref, kc_ref, o_ref, qs_ref, m_ref, l_ref, acc_ref, *, tq, tk):
    i = pl.program_id(1)
    q = q_ref[0]
    lh = _lane_head(q.shape)
    for h in range(N_HEADS):
        qs_ref[h] = jnp.where(lh == h, q, jnp.zeros_like(q))
    m_ref[...] = jnp.full(m_ref.shape, NEG, F32)
    l_ref[...] = jnp.zeros(l_ref.shape, F32)
    acc_ref[...] = jnp.zeros(acc_ref.shape, F32)
    qpos = i * tq + lax.broadcasted_iota(jnp.int32, (tq, tk), 0)
    kiota = lax.broadcasted_iota(jnp.int32, (tq, tk), 1)
    nk = (i * tq + tq + tk - 1) // tk

    def chunk(j, carry):
        off = pl.multiple_of(j * tk, tk)
        k = k_ref[0, pl.ds(off, tk), :]
        v = v_ref[0, pl.ds(off, tk), :]
        visible = (kiota + off) <= qpos
        for h in range(N_HEADS):
            s = _qk(qs_ref[h], k) - kc_ref[0, h:h + 1, pl.ds(off, tk)]
            s = jnp.where(visible, s, NEG)
            p, m_new, l_new, alpha = _online_update(s, m_ref[h], l_ref[h])
            m_ref[h] = m_new
            l_ref[h] = l_new
            acc_ref[h] = alpha * acc_ref[h] + jnp.dot(p.astype(BF16), v, preferred_element_type=F32)
        return carry

    lax.fori_loop(0, nk, chunk, 0)
    lho = _lane_head((tq, HW))
    out = jnp.zeros((tq, HW), F32)
    for h in range(N_HEADS):
        out = jnp.where(lho == h, acc_ref[h] / l_ref[h], out)
    o_ref[0] = out.astype(BF16)


def _fox_prompt(q, k, v, kcum, tq, tk):
    b, t, w = q.shape
    return pl.pallas_call(
        functools.partial(_fox_body, tq=tq, tk=tk),
        grid=(b, t // tq),
        in_specs=[pl.BlockSpec((1, tq, w), lambda bi, i: (bi, i, 0)),
                  pl.BlockSpec((1, t, w), lambda bi, i: (bi, 0, 0)),
                  pl.BlockSpec((1, t, w), lambda bi, i: (bi, 0, 0)),
                  pl.BlockSpec((1, SUB, t), lambda bi, i: (bi, 0, 0))],
        out_specs=pl.BlockSpec((1, tq, w), lambda bi, i: (bi, i, 0)),
        out_shape=jax.ShapeDtypeStruct((b, t, w), BF16),
        scratch_shapes=[pltpu.VMEM((N_HEADS, tq, w), BF16), pltpu.VMEM((N_HEADS, tq, 1), F32),
                        pltpu.VMEM((N_HEADS, tq, 1), F32), pltpu.VMEM((N_HEADS, tq, w), F32)],
        compiler_params=_cparams("arbitrary", "arbitrary"),
        name="fox_prompt",
    )(q, k, v, kcum)


def _diff_finish(acc1, l1, acc2, l2, lam, gain, out_scale):
    o = acc1 / l1 - lam * (acc2 / l2)
    return _rms(o, gain) * out_scale


def _lambda(lq1, lk1, lq2, lk2, lam_init):
    return (jnp.exp(jnp.sum(lq1 * lk1, axis=-1, keepdims=True))
            - jnp.exp(jnp.sum(lq2 * lk2, axis=-1, keepdims=True)) + lam_init)


def _diff_body(q_ref, k_ref, v_ref, lq1_ref, lk1_ref, lq2_ref, lk2_ref, sg_ref, o_ref,
               qs_ref, m_ref, l_ref, acc_ref, *, tq, tk, lam_init):
    i = pl.program_id(1)
    nmap = 2 * N_HEADS
    for half in range(2):
        q = q_ref[0, :, half * HW:(half + 1) * HW]
        lh = _lane_head(q.shape)
        for g in range(N_HEADS):
            qs_ref[half * N_HEADS + g] = jnp.where(lh == g, q, jnp.zeros_like(q))
    m_ref[...] = jnp.full(m_ref.shape, NEG, F32)
    l_ref[...] = jnp.zeros(l_ref.shape, F32)
    acc_ref[...] = jnp.zeros(acc_ref.shape, F32)
    qpos = i * tq + lax.broadcasted_iota(jnp.int32, (tq, tk), 0)
    kiota = lax.broadcasted_iota(jnp.int32, (tq, tk), 1)
    nk = (i * tq + tq + tk - 1) // tk

    def chunk(j, carry):
        off = pl.multiple_of(j * tk, tk)
        visible = (kiota + off) <= qpos
        for mp in range(nmap):
            half, head = mp // N_HEADS, mp // 2
            k = k_ref[0, pl.ds(off, tk), half * HW:(half + 1) * HW]
            v = v_ref[0, pl.ds(off, tk), head * 2 * HEAD_DIM:(head + 1) * 2 * HEAD_DIM]
            s = jnp.where(visible, _qk(qs_ref[mp], k), NEG)
            p, m_new, l_new, alpha = _online_update(s, m_ref[mp], l_ref[mp])
            m_ref[mp] = m_new
            l_ref[mp] = l_new
            acc_ref[mp] = alpha * acc_ref[mp] + jnp.dot(p.astype(BF16), v, preferred_element_type=F32)
        return carry

    lax.fori_loop(0, nk, chunk, 0)
    lam = _lambda(lq1_ref[...], lk1_ref[...], lq2_ref[...], lk2_ref[...], lam_init)
    for head in range(N_HEADS):
        o = _diff_finish(acc_ref[2 * head], l_ref[2 * head], acc_ref[2 * head + 1], l_ref[2 * head + 1],
                         lam, sg_ref[...], 1.0 - lam_init)
        o_ref[0, :, head * 2 * HEAD_DIM:(head + 1) * 2 * HEAD_DIM] = o.astype(BF16)


def _diff_prompt(q, k, v, p, lam_init, tq, tk):
    b, t, w = q.shape
    nmap = 2 * N_HEADS
    vec = _const_spec((1, HEAD_DIM))
    return pl.pallas_call(
        functools.partial(_diff_body, tq=tq, tk=tk, lam_init=lam_init),
        grid=(b, t // tq),
        in_specs=[pl.BlockSpec((1, tq, w), lambda bi, i: (bi, i, 0)),
                  pl.BlockSpec((1, t, w), lambda bi, i: (bi, 0, 0)),
                  pl.BlockSpec((1, t, w), lambda bi, i: (bi, 0, 0)),
                  vec, vec, vec, vec, _const_spec((1, 2 * HEAD_DIM))],
        out_specs=pl.BlockSpec((1, tq, w), lambda bi, i: (bi, i, 0)),
        out_shape=jax.ShapeDtypeStruct((b, t, w), BF16),
        scratch_shapes=[pltpu.VMEM((nmap, tq, HW), BF16), pltpu.VMEM((nmap, tq, 1), F32),
                        pltpu.VMEM((nmap, tq, 1), F32), pltpu.VMEM((nmap, tq, 2 * HEAD_DIM), F32)],
        compiler_params=_cparams("arbitrary", "arbitrary"),
        name="diff_prompt",
    )(q, k, v, p["lq1"], p["lk1"], p["lq2"], p["lk2"], p["c_subln"])


INT_MIN = -2 ** 31
KEY_NEG_INF = int(0xFF800000 - 2 ** 32) ^ 0x7FFFFFFF


def _sort_key(score):
    bits = lax.bitcast_convert_type(score, jnp.int32)
    return jnp.where(bits < 0, bits ^ jnp.int32(0x7FFFFFFF), bits)


def _count_rows(keys_ref, ncg, pred):
    rows = keys_ref.shape[0]
    lane = lax.broadcasted_iota(jnp.int32, (rows, 128), 1)

    def grp(c, acc):
        off = pl.multiple_of(c * 128, 128)
        kk = keys_ref[:, pl.ds(off, 128)]
        return acc + jnp.where(pred(kk, lane + off), 1.0, 0.0)

    acc = lax.fori_loop(0, ncg, grp, jnp.zeros((rows, 128), F32))
    return jnp.sum(acc, axis=-1, keepdims=True)


def _topk_select(keys_ref, ncg, topk, col_bits):
    rows = keys_ref.shape[0]
    kf = float(topk)
    cnt0 = _count_rows(keys_ref, ncg, lambda kk, col: kk >= 0)
    thr0 = jnp.where(cnt0 >= kf, jnp.int32(0), jnp.int32(INT_MIN))

    def bit_step(t, thr):
        cand = thr | jnp.left_shift(jnp.int32(1), 30 - t)
        cnt = _count_rows(keys_ref, ncg, lambda kk, col: kk >= cand)
        return jnp.where(cnt >= kf, cand, thr)

    thr = lax.fori_loop(0, 31, bit_step, thr0)
    n_gt = _count_rows(keys_ref, ncg, lambda kk, col: kk > thr)
    n_ge = _count_rows(keys_ref, ncg, lambda kk, col: kk >= thr)
    need = kf - n_gt
    excess = jnp.where((n_ge > kf) & (thr > KEY_NEG_INF), 1.0, 0.0)
    all_cols = jnp.full((rows, 1), 2 ** col_bits - 1, jnp.int32)

    def tie_search():
        def col_step(t, x):
            cand = x | jnp.left_shift(jnp.int32(1), col_bits - 1 - t)
            cnt = _count_rows(keys_ref, ncg, lambda kk, col: (kk == thr) & (col < cand))
            return jnp.where(cnt < need, cand, x)
        return lax.fori_loop(0, col_bits, col_step, jnp.zeros((rows, 1), jnp.int32))

    xcol = lax.cond(jnp.max(excess) > 0.0, tie_search, lambda: all_cols)
    return thr, xcol


def _select_bias(kk, col, thr, xcol, visible):
    tie = jnp.where(col <= xcol, 0.0, NEG)
    sel = jnp.where(kk > thr, 0.0, jnp.where(kk == thr, tie, NEG))
    return jnp.where(visible, sel, NEG)


def _dsa_body(q_ref, iq_ref, iw_ref, k_ref, v_ref, ik_ref, o_ref,
              qs_ref, iqs_ref, keys_ref, m_ref, l_ref, acc_ref, *, tq, tk, topk, col_bits):
    i = pl.program_id(1)
    q = q_ref[0]
    iq = iq_ref[0]
    lh = _lane_head(q.shape)
    for h in range(N_HEADS):
        qs_ref[h] = jnp.where(lh == h, q, jnp.zeros_like(q))
        iqs_ref[h] = jnp.where(lh == h, iq, jnp.zeros_like(iq))
    qpos = i * tq + lax.broadcasted_iota(jnp.int32, (tq, tk), 0)
    kiota = lax.broadcasted_iota(jnp.int32, (tq, tk), 1)
    nk = (i * tq + tq + tk - 1) // tk
    iw = iw_ref[0]

    def score_chunk(j, carry):
        off = pl.multiple_of(j * tk, tk)
        ik = ik_ref[0, pl.ds(off, tk), :]
        score = jnp.zeros((tq, tk), F32)
        for h in range(N_HEADS):
            sc = _qk(iqs_ref[h], ik)
            score = score + iw[:, N_HEADS + h:N_HEADS + h + 1] * jnp.maximum(sc, 0.0)
        score = jnp.where((kiota + off) <= qpos, score, -jnp.inf)
        keys_ref[:, pl.ds(off, tk)] = _sort_key(score)
        return carry

    lax.fori_loop(0, nk, score_chunk, 0)
    thr, xcol = _topk_select(keys_ref, nk * (tk // 128), topk, col_bits)

    m_ref[...] = jnp.full(m_ref.shape, NEG, F32)
    l_ref[...] = jnp.zeros(l_ref.shape, F32)
    acc_ref[...] = jnp.zeros(acc_ref.shape, F32)

    def chunk(j, carry):
        off = pl.multiple_of(j * tk, tk)
        k = k_ref[0, pl.ds(off, tk), :]
        v = v_ref[0, pl.ds(off, tk), :]
        col = kiota + off
        bias = _select_bias(keys_ref[:, pl.ds(off, tk)], col, thr, xcol, col <= qpos)
        for h in range(N_HEADS):
            s = _qk(qs_ref[h], k) + bias
            p, m_new, l_new, alpha = _online_update(s, m_ref[h], l_ref[h])
            m_ref[h] = m_new
            l_ref[h] = l_new
            acc_ref[h] = alpha * acc_ref[h] + jnp.dot(p.astype(BF16), v, preferred_element_type=F32)
        return carry

    lax.fori_loop(0, nk, chunk, 0)
    lho = _lane_head((tq, HW))
    out = jnp.zeros((tq, HW), F32)
    for h in range(N_HEADS):
        out = jnp.where(lho == h, acc_ref[h] / l_ref[h], out)
    o_ref[0] = out.astype(BF16)


def _dsa_prompt(q, iq, misc, k, v, ik, tq, tk):
    b, t, w = q.shape
    topk = min(TOPK_MAX, t // 4)
    assert tk >= topk and tk % 128 == 0
    return pl.pallas_call(
        functools.partial(_dsa_body, tq=tq, tk=tk, topk=topk, col_bits=max(t - 1, 1).bit_length()),
        grid=(b, t // tq),
        in_specs=[pl.BlockSpec((1, tq, w), lambda bi, i: (bi, i, 0)),
                  pl.BlockSpec((1, tq, w), lambda bi, i: (bi, i, 0)),
                  pl.BlockSpec((1, tq, MISC_W), lambda bi, i: (bi, i, 0)),
                  pl.BlockSpec((1, t, w), lambda bi, i: (bi, 0, 0)),
                  pl.BlockSpec((1, t, w), lambda bi, i: (bi, 0, 0)),
                  pl.BlockSpec((1, t, w), lambda bi, i: (bi, 0, 0))],
        out_specs=pl.BlockSpec((1, tq, w), lambda bi, i: (bi, i, 0)),
        out_shape=jax.ShapeDtypeStruct((b, t, w), BF16),
        scratch_shapes=[pltpu.VMEM((N_HEADS, tq, w), BF16), pltpu.VMEM((N_HEADS, tq, w), BF16),
                        pltpu.VMEM((tq, t), jnp.int32),
                        pltpu.VMEM((N_HEADS, tq, 1), F32), pltpu.VMEM((N_HEADS, tq, 1), F32),
                        pltpu.VMEM((N_HEADS, tq, w), F32)],
        compiler_params=_cparams("arbitrary", "arbitrary"),
        name="dsa_prompt",
    )(q, iq, misc, k, v, ik)


def _post_body(x_ref, oa_ref, ob_ref, oc_ref, g_ref, wgate_ref, bgate_ref, woa_ref, wob_ref, woc_ref, wout_ref,
               o_ref, acc_ref, *, mc):
    x = x_ref[...]
    d = x.shape[1]
    h = _rms(x, g_ref[...]).astype(BF16)
    oa, ob, oc = oa_ref[...], ob_ref[...], oc_ref[...]
    for c in range(d // mc):
        merged = None
        for br, (o, w_ref) in enumerate(((oa, woa_ref), (ob, wob_ref), (oc, woc_ref))):
            sl = slice(br * d + c * mc, br * d + (c + 1) * mc)
            gate = _sigmoid(jnp.dot(h, wgate_ref[:, sl], preferred_element_type=F32) + bgate_ref[:, sl])
            term = gate * jnp.dot(o, w_ref[:, c * mc:(c + 1) * mc], preferred_element_type=F32)
            merged = term if merged is None else merged + term
        part = jnp.dot(merged.astype(BF16), wout_ref[c * mc:(c + 1) * mc, :], preferred_element_type=F32)
        if c == 0:
            acc_ref[...] = part
        else:
            acc_ref[...] += part
    o_ref[...] = x + acc_ref[...]


def _post(x, oa, ob, oc, p, tm):
    n, d = x.shape
    row = lambda w: pl.BlockSpec((tm, w), lambda i: (i, 0))
    return pl.pallas_call(
        functools.partial(_post_body, mc=256),
        grid=(n // tm,),
        in_specs=[row(d), row(HW), row(HW), row(CW), _const_spec((1, d)),
                  _const_spec((d, 3 * d)), _const_spec((1, 3 * d)),
                  _const_spec((HW, d)), _const_spec((HW, d)), _const_spec((CW, d)), _const_spec((d, d))],
        out_specs=row(d),
        out_shape=jax.ShapeDtypeStruct((n, d), F32),
        scratch_shapes=[pltpu.VMEM((tm, d), F32)],
        compiler_params=_cparams("arbitrary"),
        name="post",
    )(x, oa, ob, oc, p["norm_mix"], p["w_gate"], p["b_gate"], p["w_o_a"], p["w_o_b"], p["w_o_c"], p["w_out"])


def _expand_rows(x, reps):
    return jnp.concatenate([jnp.broadcast_to(x[g:g + 1], (SUB, x.shape[1])) for g in range(reps)], axis=0)


def _sidx_body(pt_ref, iq_ref, iw_ref, page_ref, new_ref, bias_ref, keys_ref, *, npg, n_new, topk, col_bits):
    p = pl.program_id(1)

    def scores(kmat):
        sc = jnp.maximum(_qk(iq_ref[0], kmat), 0.0) * iw_ref[0]
        return sc[0:SUB] + sc[SUB:2 * SUB] + sc[2 * SUB:3 * SUB] + sc[3 * SUB:4 * SUB]

    off = pl.multiple_of(p * PAGE_SIZE, PAGE_SIZE)
    keys_ref[:, pl.ds(off, PAGE_SIZE)] = _sort_key(scores(page_ref[...].astype(BF16)))

    @pl.when(p == npg - 1)
    def _():
        r = lax.broadcasted_iota(jnp.int32, (SUB, PAGE_SIZE), 0)
        c = lax.broadcasted_iota(jnp.int32, (SUB, PAGE_SIZE), 1)
        vis_new = (c <= r) & (c < n_new)
        s_new = jnp.where(vis_new, scores(new_ref[0]), -jnp.inf)
        keys_ref[:, npg * PAGE_SIZE:(npg + 1) * PAGE_SIZE] = _sort_key(s_new)
        thr, xcol = _topk_select(keys_ref, npg + 1, topk, col_bits)
        for g in range(npg + 1):
            col = c + g * PAGE_SIZE
            vis = vis_new if g == npg else jnp.full((SUB, PAGE_SIZE), True)
            bias_ref[0, :, g * PAGE_SIZE:(g + 1) * PAGE_SIZE] = _select_bias(
                keys_ref[:, g * PAGE_SIZE:(g + 1) * PAGE_SIZE], col, thr, xcol, vis)


def _sample_index(pt, iq_rows, iw_rows, cache_idx, new_idx, layer, n_new):
    nb = iq_rows.shape[0]
    npg = pt.shape[0] // nb
    tk_total = npg * PAGE_SIZE + n_new
    topk = min(TOPK_MAX, tk_total // 4)
    ncols = (npg + 1) * PAGE_SIZE
    return pl.pallas_call(
        functools.partial(_sidx_body, npg=npg, n_new=n_new, topk=topk, col_bits=max(ncols - 1, 1).bit_length()),
        grid_spec=pltpu.PrefetchScalarGridSpec(
            num_scalar_prefetch=1,
            grid=(nb, npg),
            in_specs=[pl.BlockSpec((1, N_HEADS * SUB, HEAD_DIM), lambda b, p, pt: (b, 0, 0)),
                      pl.BlockSpec((1, N_HEADS * SUB, 1), lambda b, p, pt: (b, 0, 0)),
                      pl.BlockSpec((None, None, PAGE_SIZE, HEAD_DIM),
                                   lambda b, p, pt: (layer, pt[b * npg + p], 0, 0)),
                      pl.BlockSpec((1, PAGE_SIZE, HEAD_DIM), lambda b, p, pt: (b, 0, 0))],
            out_specs=pl.BlockSpec((1, SUB, ncols), lambda b, p, pt: (b, 0, 0)),
            scratch_shapes=[pltpu.VMEM((SUB, ncols), jnp.int32)]),
        out_shape=jax.ShapeDtypeStruct((nb, SUB, ncols), F32),
        compiler_params=_cparams("arbitrary", "arbitrary"),
        name="sample_index",
    )(pt, iq_rows, iw_rows, cache_idx, new_idx)


def _sattn_body(pt_ref, aq_ref, bq_ref, cq_ref, bias_ref,
                ak_ref, av_ref, lf_ref, bk_ref, bv_ref, ck_ref, cv_ref,
                akn_ref, avn_ref, lfn_ref, bkn_ref, bvn_ref, ckn_ref, cvn_ref,
                lq1_ref, lk1_ref, lq2_ref, lk2_ref, sg_ref,
                oa_ref, ob_ref, oc_ref,
                ma_ref, la_ref, acca_ref, mb_ref, lb_ref, accb_ref, mc_ref, lc_ref, accc_ref, ctot_ref,
                *, npg, n_new, lam_init):
    p = pl.program_id(1)
    nmap = 2 * N_HEADS

    @pl.when(p == 0)
    def _():
        for m_ref, l_ref, acc_ref in ((ma_ref, la_ref, acca_ref), (mb_ref, lb_ref, accb_ref),
                                      (mc_ref, lc_ref, accc_ref)):
            m_ref[...] = jnp.full(m_ref.shape, NEG, F32)
            l_ref[...] = jnp.zeros(l_ref.shape, F32)
            acc_ref[...] = jnp.zeros(acc_ref.shape, F32)
        ctot_ref[...] = jnp.zeros(ctot_ref.shape, F32)

    r = lax.broadcasted_iota(jnp.int32, (PAGE_SIZE, PAGE_SIZE), 0)
    c = lax.broadcasted_iota(jnp.int32, (PAGE_SIZE, PAGE_SIZE), 1)
    tri = jnp.where(r <= c, 1.0, 0.0).astype(F32)

    def step(ak, av, lf, bk, bv, ck, cv, vis):
        lf = jnp.concatenate([lf, jnp.zeros((SUB - N_HEADS, PAGE_SIZE), F32)], axis=0)
        cum = jnp.dot(lf, tri, preferred_element_type=F32, precision=lax.Precision.HIGHEST) + ctot_ref[...]
        ctot_ref[...] = cum[:, PAGE_SIZE - 1:PAGE_SIZE]
        s = _qk(aq_ref[0], ak) - _expand_rows(cum, N_HEADS)
        if vis is not None:
            s = jnp.where(jnp.concatenate([vis] * N_HEADS, axis=0), s, NEG)
        pa, m_new, l_new, alpha = _online_update(s, ma_ref[...], la_ref[...])
        ma_ref[...] = m_new
        la_ref[...] = l_new
        acca_ref[...] = alpha * acca_ref[...] + jnp.dot(pa.astype(BF16), av, preferred_element_type=F32)
        s = _qk(bq_ref[0], bk) + jnp.concatenate([bias_ref[0]] * N_HEADS, axis=0)
        pb, m_new, l_new, alpha = _online_update(s, mb_ref[...], lb_ref[...])
        mb_ref[...] = m_new
        lb_ref[...] = l_new
        accb_ref[...] = alpha * accb_ref[...] + jnp.dot(pb.astype(BF16), bv, preferred_element_type=F32)
        s = _qk(cq_ref[0], ck)
        if vis is not None:
            s = jnp.where(jnp.concatenate([vis] * nmap, axis=0), s, NEG)
        pc, m_new, l_new, alpha = _online_update(s, mc_ref[...], lc_ref[...])
        mc_ref[...] = m_new
        lc_ref[...] = l_new
        pc = pc.astype(BF16)
        for head in range(N_HEADS):
            rows = slice(head * 2 * SUB, (head + 1) * 2 * SUB)
            accc_ref[rows, :] = alpha[rows] * accc_ref[rows, :] + jnp.dot(
                pc[rows], cv[:, head * 2 * HEAD_DIM:(head + 1) * 2 * HEAD_DIM], preferred_element_type=F32)

    @pl.when(p < npg)
    def _():
        step(ak_ref[...].astype(BF16), av_ref[...].astype(BF16), lf_ref[...],
             bk_ref[...].astype(BF16), bv_ref[...].astype(BF16),
             ck_ref[...].astype(BF16), cv_ref[...].astype(BF16), None)

    @pl.when(p == npg)
    def _():
        rr = lax.broadcasted_iota(jnp.int32, (SUB, PAGE_SIZE), 0)
        cc = lax.broadcasted_iota(jnp.int32, (SUB, PAGE_SIZE), 1)
        vis = (cc <= rr) & (cc < n_new)
        step(akn_ref[0], avn_ref[0], lfn_ref[0], bkn_ref[0], bvn_ref[0], ckn_ref[0], cvn_ref[0], vis)
        lho = _lane_head((SUB, HW))
        for acc_ref, l_ref, o_ref in ((acca_ref, la_ref, oa_ref), (accb_ref, lb_ref, ob_ref)):
            o = acc_ref[...] / l_ref[...]
            out = jnp.zeros((SUB, HW), F32)
            for h in range(N_HEADS):
                out = jnp.where(lho == h, o[h * SUB:(h + 1) * SUB], out)
            o_ref[0] = out.astype(BF16)
        lam = _lambda(lq1_ref[...], lk1_ref[...], lq2_ref[...], lk2_ref[...], lam_init)
        accc = accc_ref[...]
        lc = lc_ref[...]
        for head in range(N_HEADS):
            r1 = slice(head * 2 * SUB, head * 2 * SUB + SUB)
            r2 = slice(head * 2 * SUB + SUB, (head + 1) * 2 * SUB)
            o = _diff_finish(accc[r1], lc[r1], accc[r2], lc[r2], lam, sg_ref[...], 1.0 - lam_init)
            oc_ref[0, :, head * 2 * HEAD_DIM:(head + 1) * 2 * HEAD_DIM] = o.astype(BF16)


def _sample_attention(pt, q_rows, bias, caches, new_rows, p, layer, lam_init, n_new):
    aq, bq, cq = q_rows
    nb = aq.shape[0]
    npg = pt.shape[0] // nb
    nmap = 2 * N_HEADS

    def page(shape):
        nd = len(shape)
        return pl.BlockSpec((None, None) + shape,
                            lambda b, p, pt: (layer, pt[b * npg + jnp.minimum(p, npg - 1)]) + (0,) * nd)

    per_seq = lambda shape: pl.BlockSpec((1,) + shape, lambda b, p, pt: (b,) + (0,) * len(shape))
    const = lambda shape: pl.BlockSpec(shape, lambda b, p, pt: (0,) * len(shape))
    in_specs = [per_seq((N_HEADS * SUB, HW)), per_seq((N_HEADS * SUB, HW)), per_seq((nmap * SUB, CW)),
                pl.BlockSpec((1, SUB, PAGE_SIZE), lambda b, p, pt: (b, 0, p)),
                page((PAGE_SIZE, HW)), page((PAGE_SIZE, HW)), page((N_HEADS, PAGE_SIZE)),
                page((PAGE_SIZE, HW)), page((PAGE_SIZE, HW)), page((PAGE_SIZE, CW)), page((PAGE_SIZE, CW)),
                per_seq((PAGE_SIZE, HW)), per_seq((PAGE_SIZE, HW)), per_seq((N_HEADS, PAGE_SIZE)),
                per_seq((PAGE_SIZE, HW)), per_seq((PAGE_SIZE, HW)), per_seq((PAGE_SIZE, CW)),
                per_seq((PAGE_SIZE, CW)),
                const((1, HEAD_DIM)), const((1, HEAD_DIM)), const((1, HEAD_DIM)), const((1, HEAD_DIM)),
                const((1, 2 * HEAD_DIM))]
    out_specs = [per_seq((SUB, HW)), per_seq((SUB, HW)), per_seq((SUB, CW))]
    scratch = [pltpu.VMEM((N_HEADS * SUB, 1), F32), pltpu.VMEM((N_HEADS * SUB, 1), F32),
               pltpu.VMEM((N_HEADS * SUB, HW), F32),
               pltpu.VMEM((N_HEADS * SUB, 1), F32), pltpu.VMEM((N_HEADS * SUB, 1), F32),
               pltpu.VMEM((N_HEADS * SUB, HW), F32),
               pltpu.VMEM((nmap * SUB, 1), F32), pltpu.VMEM((nmap * SUB, 1), F32),
               pltpu.VMEM((nmap * SUB, 2 * HEAD_DIM), F32),
               pltpu.VMEM((SUB, 1), F32)]
    return pl.pallas_call(
        functools.partial(_sattn_body, npg=npg, n_new=n_new, lam_init=lam_init),
        grid_spec=pltpu.PrefetchScalarGridSpec(
            num_scalar_prefetch=1, grid=(nb, npg + 1),
            in_specs=in_specs, out_specs=out_specs, scratch_shapes=scratch),
        out_shape=[jax.ShapeDtypeStruct((nb, SUB, HW), BF16), jax.ShapeDtypeStruct((nb, SUB, HW), BF16),
                   jax.ShapeDtypeStruct((nb, SUB, CW), BF16)],
        compiler_params=_cparams("arbitrary", "arbitrary"),
        name="sample_attention",
    )(pt, aq, bq, cq, bias, *caches, *new_rows, p["lq1"], p["lk1"], p["lq2"], p["lk2"], p["c_subln"])


def _rope_tables(pos):
    half = ROPE_DIM // 2
    inv_freq = ROPE_THETA ** (-jnp.arange(half, dtype=F32) / half)
    ang = pos.astype(F32)[:, None] * inv_freq[None, :]
    cos, sin = jnp.cos(ang), jnp.sin(ang)
    n = pos.shape[0]
    one = jnp.ones((n, HEAD_DIM - ROPE_DIM), F32)
    zero = jnp.zeros((n, HEAD_DIM - ROPE_DIM), F32)
    z8 = jnp.zeros((n, half), F32)
    c = jnp.concatenate([cos, cos, one], axis=1)
    sa = jnp.concatenate([z8, sin, zero], axis=1)
    sb = jnp.concatenate([-sin, z8, zero], axis=1)
    return tuple(jnp.tile(t, (1, 2)) for t in (c, sa, sb))


def _masked_query_rows(q, n_seq, n_new, n_groups):
    w = q.shape[1]
    q = q.reshape(n_seq, 1, n_new, w)
    q = jnp.pad(q, ((0, 0), (0, 0), (0, SUB - n_new), (0, 0)))
    grp = (jnp.arange(w) // HEAD_DIM)[None, :] == jnp.arange(n_groups)[:, None]
    q = jnp.where(grp[None, :, None, :], q, jnp.zeros((), q.dtype))
    return q.reshape(n_seq, n_groups * SUB, w)


def _pad_keys(x, n_seq, n_new):
    x = x.reshape(n_seq, n_new, x.shape[1])
    return jnp.pad(x, ((0, 0), (0, PAGE_SIZE - n_new), (0, 0)))


def _layer_params(l, a):
    bf = lambda x: x.astype(BF16)
    row = lambda x: x.reshape(1, -1)
    d = a["norm_mix"].shape[1]
    fox_bias = jnp.zeros((1, MISC_W), F32).at[0, :N_HEADS].set(a["fox_bias_f"][l])
    blk = jnp.arange(HW) // HEAD_DIM
    bd = jnp.where(blk[:, None] == blk[None, :], 1.0 / HEAD_DIM, 0.0).astype(BF16)
    return {
        "norm_ffn1": row(a["norm_ffn1"][l]), "norm_mix": row(a["norm_mix"][l]), "norm_ffn2": row(a["norm_ffn2"][l]),
        "ffn1": (bf(a["ffn1_w_gate"][l]), bf(a["ffn1_w_up"][l]), bf(a["ffn1_w_down"][l])),
        "ffn2": (bf(a["ffn2_w_gate"][l]), bf(a["ffn2_w_up"][l]), bf(a["ffn2_w_down"][l])),
        "w_in": bf(_pack_w_in(a["w_in"][l])), "fox_bias": fox_bias, "bd": bd,
        "gaq": row(jnp.tile(a["a_q_norm"][l], N_HEADS)), "gak": row(jnp.tile(a["a_k_norm"][l], N_HEADS)),
        "gbq": row(jnp.tile(a["b_q_norm"][l], N_HEADS)), "gbk": row(jnp.tile(a["b_k_norm"][l], N_HEADS)),
        "gcq": row(jnp.tile(a["c_q_norm"][l].reshape(-1), N_HEADS)),
        "gck": row(jnp.tile(a["c_k_norm"][l].reshape(-1), N_HEADS)),
        "lq1": row(a["c_lambda_q1"][l]), "lk1": row(a["c_lambda_k1"][l]),
        "lq2": row(a["c_lambda_q2"][l]), "lk2": row(a["c_lambda_k2"][l]),
        "c_subln": row(a["c_subln"][l]),
        "w_gate": bf(a["w_gate"][l]), "b_gate": row(a["b_gate"][l]),
        "w_o_a": bf(a["w_o_a"][l]), "w_o_b": bf(a["w_o_b"][l]), "w_o_c": bf(a["w_o_c"][l]),
        "w_out": bf(a["w_out"][l]),
    }


_INPROJ_NAMES = ("aq", "akf", "akb", "avf", "avb", "bq", "bkf", "bkb", "bvf", "bvb", "biq", "bif", "bib",
                 "cq", "ckf", "ckb", "cvf", "cvb", "misc", "cum")


def kernel(x_prompt, x_sample, cache_a_k, cache_a_v, cache_a_logf, cache_b_k, cache_b_v, cache_b_idx, cache_c_k, cache_c_v, page_table, norm_ffn1, ffn1_w_gate, ffn1_w_up, ffn1_w_down, norm_mix, w_in, fox_bias_f, a_q_norm, a_k_norm, b_q_norm, b_k_norm, c_q_norm, c_k_norm, c_lambda_q1, c_lambda_k1, c_lambda_q2, c_lambda_k2, c_subln, w_o_a, w_o_b, w_o_c, w_gate, b_gate, w_out, norm_ffn2, ffn2_w_gate, ffn2_w_up, ffn2_w_down):
    a = dict(norm_ffn1=norm_ffn1, ffn1_w_gate=ffn1_w_gate, ffn1_w_up=ffn1_w_up, ffn1_w_down=ffn1_w_down,
             norm_mix=norm_mix, w_in=w_in, fox_bias_f=fox_bias_f, a_q_norm=a_q_norm, a_k_norm=a_k_norm,
             b_q_norm=b_q_norm, b_k_norm=b_k_norm, c_q_norm=c_q_norm, c_k_norm=c_k_norm,
             c_lambda_q1=c_lambda_q1, c_lambda_k1=c_lambda_k1, c_lambda_q2=c_lambda_q2, c_lambda_k2=c_lambda_k2,
             c_subln=c_subln, w_o_a=w_o_a, w_o_b=w_o_b, w_o_c=w_o_c, w_gate=w_gate, b_gate=b_gate, w_out=w_out,
             norm_ffn2=norm_ffn2, ffn2_w_gate=ffn2_w_gate, ffn2_w_up=ffn2_w_up, ffn2_w_down=ffn2_w_down)
    depth = w_in.shape[0]
    bsz, seq, d = x_prompt.shape
    nb, n_new, _ = x_sample.shape
    n_pool = cache_a_k.shape[1]
    npg = page_table.shape[1]
    past_len = npg * PAGE_SIZE
    np_tok, ns_tok = bsz * seq, nb * n_new
    tm_p = min(512, seq)
    tm_s = min(512, ns_tok)
    tq = min(256, seq)
    tk = min(512, seq)

    caches = (cache_a_k.reshape(depth, n_pool, PAGE_SIZE, HW), cache_a_v.reshape(depth, n_pool, PAGE_SIZE, HW),
              jnp.swapaxes(cache_a_logf, 2, 3),
              cache_b_k.reshape(depth, n_pool, PAGE_SIZE, HW), cache_b_v.reshape(depth, n_pool, PAGE_SIZE, HW),
              cache_c_k.reshape(depth, n_pool, PAGE_SIZE, CW), cache_c_v.reshape(depth, n_pool, PAGE_SIZE, CW))
    pt = page_table.reshape(-1).astype(jnp.int32)

    tabs_p = _rope_tables(jnp.arange(seq, dtype=jnp.int32))
    pos_s = past_len + jnp.arange(n_new, dtype=jnp.int32)
    tabs_s = _rope_tables(jnp.tile(pos_s, tm_s // n_new))

    xp = x_prompt.reshape(np_tok, d)
    xs = x_sample.reshape(ns_tok, d)
    rows_p, rows_s = [], []
    for l in range(depth):
        p = _layer_params(l, a)
        lam_init = 0.8 - 0.6 * math.exp(-0.3 * l)

        xp = _ffn(xp, p["norm_ffn1"], *p["ffn1"], tm_p)
        pr = dict(zip(_INPROJ_NAMES, _inproj(xp, p, tabs_p, tm_p, seq)))
        b3 = lambda z: z.reshape(bsz, seq, z.shape[1])
        kcum = jnp.swapaxes(b3(pr["cum"])[:, :, :SUB], 1, 2)
        oa = _fox_prompt(b3(pr["aq"]), b3(pr["akb"]), b3(pr["avb"]), kcum, tq, tk)
        ob = _dsa_prompt(b3(pr["bq"]), b3(pr["biq"]), b3(pr["misc"]), b3(pr["bkb"]), b3(pr["bvb"]),
                         b3(pr["bib"]), tq, tk)
        oc = _diff_prompt(b3(pr["cq"]), b3(pr["ckb"]), b3(pr["cvb"]), p, lam_init, tq, tk)
        xp = _post(xp, oa.reshape(np_tok, HW), ob.reshape(np_tok, HW), oc.reshape(np_tok, CW), p, tm_p)
        xp = _ffn(xp, p["norm_ffn2"], *p["ffn2"], tm_p)
        rows_p.append(pr)

        xs = _ffn(xs, p["norm_ffn1"], *p["ffn1"], tm_s)
        sr = dict(zip(_INPROJ_NAMES, _inproj(xs, p, tabs_s, tm_s, ns_tok)))
        iq_rows = _masked_query_rows(sr["biq"], nb, n_new, N_HEADS)
        iq_rows = iq_rows.reshape(nb, N_HEADS, SUB, N_HEADS, HEAD_DIM).sum(axis=3).astype(BF16)
        iq_rows = iq_rows.reshape(nb, N_HEADS * SUB, HEAD_DIM)
        iw = sr["misc"][:, N_HEADS:2 * N_HEADS].reshape(nb, n_new, N_HEADS)
        iw = jnp.pad(jnp.swapaxes(iw, 1, 2), ((0, 0), (0, 0), (0, SUB - n_new)))
        iw_rows = iw.reshape(nb, N_HEADS * SUB, 1)
        new_idx = _pad_keys(sr["bib"][:, :HEAD_DIM], nb, n_new)
        bias = _sample_index(pt, iq_rows, iw_rows, cache_b_idx, new_idx, l, n_new)
        q_rows = (_masked_query_rows(sr["aq"], nb, n_new, N_HEADS),
                  _masked_query_rows(sr["bq"], nb, n_new, N_HEADS),
                  _masked_query_rows(sr["cq"], nb, n_new, 2 * N_HEADS))
        lf_new = jnp.swapaxes(sr["misc"][:, :N_HEADS].reshape(nb, n_new, N_HEADS), 1, 2)
        lf_new = jnp.pad(lf_new, ((0, 0), (0, 0), (0, PAGE_SIZE - n_new)))
        new_rows = (_pad_keys(sr["akb"], nb, n_new), _pad_keys(sr["avb"], nb, n_new), lf_new,
                    _pad_keys(sr["bkb"], nb, n_new), _pad_keys(sr["bvb"], nb, n_new),
                    _pad_keys(sr["ckb"], nb, n_new), _pad_keys(sr["cvb"], nb, n_new))
        soa, sob, soc = _sample_attention(pt, q_rows, bias, caches, new_rows, p, l, lam_init, n_new)
        take = lambda o: o[:, :n_new].reshape(ns_tok, o.shape[2])
        xs = _post(xs, take(soa), take(sob), take(soc), p, tm_s)
        xs = _ffn(xs, p["norm_ffn2"], *p["ffn2"], tm_s)
        rows_s.append(sr)

    def stack(rows, name, lead, tail):
        z = jnp.stack([r[name] for r in rows])
        if name == "misc":
            z = z[:, :, :N_HEADS]
        return z.reshape((depth,) + lead + tail)

    lead_p, lead_s = (bsz, seq), (nb, n_new)
    hd = (N_HEADS, HEAD_DIM)
    out = [xp.reshape(bsz, seq, d), xs.reshape(nb, n_new, d)]
    for name, tail in (("akf", hd), ("avf", hd), ("misc", (N_HEADS,)), ("bkf", hd), ("bvf", hd),
                       ("bif", (HEAD_DIM,)), ("ckf", (N_HEADS, 2, HEAD_DIM)), ("cvf", (N_HEADS, 2 * HEAD_DIM))):
        out.append(stack(rows_p, name, lead_p, tail))
        out.append(stack(rows_s, name, lead_s, tail))
    return tuple(out)
```

```python
import functools
import math

import jax
import jax.numpy as jnp
from jax import lax
from jax.experimental import pallas as pl
from jax.experimental.pallas import tpu as pltpu

F32 = jnp.float32
BF16 = jnp.bfloat16

HEAD_DIM = 64
ROPE_DIM = HEAD_DIM // 4
ROPE_THETA = 500000.0
NORM_EPS = 1e-6
TOPK_MAX = 256
PAGE_SIZE = 128
N_HEADS = 4
HW = N_HEADS * HEAD_DIM
CW = 2 * HW
MISC_W = 128
NEG = -1e30
SUB = 8
VMEM_LIMIT = 56 * 1024 * 1024


def _cparams(*sem):
    return pltpu.CompilerParams(dimension_semantics=sem, vmem_limit_bytes=VMEM_LIMIT)


def _const_spec(shape):
    nd = len(shape)
    return pl.BlockSpec(shape, lambda *_: (0,) * nd, pipeline_mode=pl.Buffered(1))


def _rms(x, g):
    ms = jnp.mean(x * x, axis=-1, keepdims=True)
    return x * lax.rsqrt(ms + NORM_EPS) * g


def _sigmoid(x):
    return 1.0 / (1.0 + jnp.exp(-x))


def _lane_head(shape, width=HEAD_DIM):
    return lax.broadcasted_iota(jnp.int32, shape, len(shape) - 1) // width


def _ffn_body(x_ref, g_ref, wg_ref, wu_ref, wd_ref, o_ref, acc_ref, *, fc):
    x = x_ref[...]
    h = _rms(x, g_ref[...]).astype(BF16)
    dff = wg_ref.shape[1]
    for c in range(dff // fc):
        sl = slice(c * fc, (c + 1) * fc)
        g = jnp.dot(h, wg_ref[:, sl], preferred_element_type=F32)
        u = jnp.dot(h, wu_ref[:, sl], preferred_element_type=F32)
        a = (g * _sigmoid(g) * u).astype(BF16)
        d = jnp.dot(a, wd_ref[sl, :], preferred_element_type=F32)
        if c == 0:
            acc_ref[...] = d
        else:
            acc_ref[...] += d
    o_ref[...] = x + 0.5 * acc_ref[...]


def _ffn(x, g, wg, wu, wd, tm):
    n, d = x.shape
    dff = wg.shape[1]
    return pl.pallas_call(
        functools.partial(_ffn_body, fc=256),
        grid=(n // tm,),
        in_specs=[pl.BlockSpec((tm, d), lambda i: (i, 0)),
                  _const_spec((1, d)), _const_spec((d, dff)), _const_spec((d, dff)), _const_spec((dff, d))],
        out_specs=pl.BlockSpec((tm, d), lambda i: (i, 0)),
        out_shape=jax.ShapeDtypeStruct((n, d), F32),
        scratch_shapes=[pltpu.VMEM((tm, d), F32)],
        compiler_params=_cparams("arbitrary"),
        name="ffn",
    )(x, g, wg, wu, wd)


_O_AQ, _O_AK, _O_AV = 0, HW, 2 * HW
_O_BQ, _O_BK, _O_BV, _O_BIQ, _O_BIK = 3 * HW, 4 * HW, 5 * HW, 6 * HW, 7 * HW
_O_CQ, _O_CK, _O_CV = 8 * HW, 8 * HW + CW, 8 * HW + 2 * CW
_O_MISC = 8 * HW + 3 * CW
PACK_W = _O_MISC + MISC_W


def _pack_w_in(w):
    widths = (HW, HW, HW, N_HEADS, HW, HW, HW, HW, HEAD_DIM, N_HEADS, CW, CW, CW)
    offs = [0]
    for wd in widths:
        offs.append(offs[-1] + wd)
    part = lambda i: w[:, offs[i]:offs[i + 1]]
    aq, ak, av, af, bq, bk, bv, biq, bik, biw, cq, ck, cv = [part(i) for i in range(13)]
    misc = jnp.concatenate([af, biw, jnp.zeros((w.shape[0], MISC_W - 2 * N_HEADS), w.dtype)], axis=1)
    return jnp.concatenate([aq, ak, av, bq, bk, bv, biq, jnp.tile(bik, (1, N_HEADS)), cq, ck, cv, misc], axis=1)


def _headnorm(x, bd_ref, g):
    sq = x * x
    hi = sq.astype(BF16)
    lo = (sq - hi.astype(F32)).astype(BF16)
    ms = (jnp.dot(hi, bd_ref[...], preferred_element_type=F32)
          + jnp.dot(lo, bd_ref[...], preferred_element_type=F32))
    return x * lax.rsqrt(ms + NORM_EPS) * g


def _rope(x, cos, sa, sb):
    w = x.shape[1]
    rep = w // cos.shape[1]
    if rep > 1:
        cos, sa, sb = (jnp.concatenate([t] * rep, axis=1) for t in (cos, sa, sb))
    half = ROPE_DIM // 2
    return x * cos + pltpu.roll(x, half, axis=1) * sa + pltpu.roll(x, w - half, axis=1) * sb


def _inproj_body(x_ref, g_ref, w_ref, fb_ref, gaq_ref, gak_ref, gbq_ref, gbk_ref, gcq_ref, gck_ref,
                 cos_ref, sa_ref, sb_ref, bd_ref,
                 aq_o, akf_o, akb_o, avf_o, avb_o, bq_o, bkf_o, bkb_o, bvf_o, bvb_o, biq_o, bif_o, bib_o,
                 cq_o, ckf_o, ckb_o, cvf_o, cvb_o, misc_o, cum_o, carry_ref, *, tiles_per_seq, scale):
    i = pl.program_id(0)
    x = x_ref[...]
    h = _rms(x, g_ref[...]).astype(BF16)
    proj = lambda off, wd: jnp.dot(h, w_ref[:, off:off + wd], preferred_element_type=F32)
    cos, sa, sb = cos_ref[...], sa_ref[...], sb_ref[...]

    aq = _headnorm(proj(_O_AQ, HW), bd_ref, gaq_ref[...])
    aq_o[...] = (aq * scale).astype(BF16)
    ak = _headnorm(proj(_O_AK, HW), bd_ref, gak_ref[...])
    akf_o[...] = ak
    akb_o[...] = ak.astype(BF16)
    av = proj(_O_AV, HW)
    avf_o[...] = av
    avb_o[...] = av.astype(BF16)

    bq = _rope(_headnorm(proj(_O_BQ, HW), bd_ref, gbq_ref[...]), cos, sa, sb)
    bq_o[...] = (bq * scale).astype(BF16)
    bk = _rope(_headnorm(proj(_O_BK, HW), bd_ref, gbk_ref[...]), cos, sa, sb)
    bkf_o[...] = bk
    bkb_o[...] = bk.astype(BF16)
    bv = proj(_O_BV, HW)
    bvf_o[...] = bv
    bvb_o[...] = bv.astype(BF16)
    biq_o[...] = _rope(proj(_O_BIQ, HW), cos, sa, sb).astype(BF16)
    bik = _rope(proj(_O_BIK, HW), cos, sa, sb)
    bif_o[...] = bik[:, :HEAD_DIM]
    bib_o[...] = bik.astype(BF16)

    for half in range(2):
        sl = slice(half * HW, (half + 1) * HW)
        cq = _rope(_headnorm(proj(_O_CQ + half * HW, HW), bd_ref, gcq_ref[:, sl]), cos, sa, sb)
        cq_o[:, sl] = (cq * scale).astype(BF16)
        ck = _rope(_headnorm(proj(_O_CK + half * HW, HW), bd_ref, gck_ref[:, sl]), cos, sa, sb)
        ckf_o[:, sl] = ck
        ckb_o[:, sl] = ck.astype(BF16)
    cv = proj(_O_CV, CW)
    cvf_o[...] = cv
    cvb_o[...] = cv.astype(BF16)

    misc = proj(_O_MISC, MISC_W)
    z = misc + fb_ref[...]
    logf = jnp.minimum(z, 0.0) - jnp.log(1.0 + jnp.exp(-jnp.abs(z)))
    lane = lax.broadcasted_iota(jnp.int32, misc.shape, 1)
    misc = jnp.where(lane < N_HEADS, logf, misc)
    misc_o[...] = misc

    tm = misc.shape[0]
    r = lax.broadcasted_iota(jnp.int32, (tm, tm), 0)
    c = lax.broadcasted_iota(jnp.int32, (tm, tm), 1)
    tri = jnp.where(c <= r, 1.0, 0.0).astype(F32)

    @pl.when(i % tiles_per_seq == 0)
    def _():
        carry_ref[...] = jnp.zeros_like(carry_ref)

    cum = jnp.dot(tri, misc, preferred_element_type=F32, precision=lax.Precision.HIGHEST) + carry_ref[...]
    cum_o[...] = cum
    carry_ref[...] = cum[tm - 1:tm, :]


def _inproj(x, p, tabs, tm, seq_len):
    n, d = x.shape
    cos, sa, sb = tabs
    tab_tiles = cos.shape[0] // tm
    row = lambda w: pl.BlockSpec((tm, w), lambda i: (i, 0))
    tab = pl.BlockSpec((tm, cos.shape[1]), lambda i: (i % tab_tiles, 0))
    sds = lambda w, dt: jax.ShapeDtypeStruct((n, w), dt)
    outs = [
        (HW, BF16),
        (HW, F32), (HW, BF16), (HW, F32), (HW, BF16),
        (HW, BF16),
        (HW, F32), (HW, BF16), (HW, F32), (HW, BF16),
        (HW, BF16),
        (HEAD_DIM, F32), (HW, BF16),
        (CW, BF16),
        (CW, F32), (CW, BF16), (CW, F32), (CW, BF16),
        (MISC_W, F32), (MISC_W, F32),
    ]
    return pl.pallas_call(
        functools.partial(_inproj_body, tiles_per_seq=max(seq_len // tm, 1), scale=HEAD_DIM ** -0.5),
        grid=(n // tm,),
        in_specs=[row(d), _const_spec((1, d)), _const_spec((d, PACK_W)), _const_spec((1, MISC_W)),
                  _const_spec((1, HW)), _const_spec((1, HW)), _const_spec((1, HW)), _const_spec((1, HW)),
                  _const_spec((1, CW)), _const_spec((1, CW)), tab, tab, tab, _const_spec((HW, HW))],
        out_specs=[row(w) for w, _ in outs],
        out_shape=[sds(w, dt) for w, dt in outs],
        scratch_shapes=[pltpu.VMEM((1, MISC_W), F32)],
        compiler_params=_cparams("arbitrary"),
        name="inproj",
    )(x, p["norm_mix"], p["w_in"], p["fox_bias"], p["gaq"], p["gak"], p["gbq"], p["gbk"], p["gcq"], p["gck"],
      cos, sa, sb, p["bd"])


def _online_update(s, m_prev, l_prev):
    m_new = jnp.maximum(m_prev, jnp.max(s, axis=-1, keepdims=True))
    alpha = jnp.exp(m_prev - m_new)
    p = jnp.exp(s - m_new)
    l_new = alpha * l_prev + jnp.sum(p, axis=-1, keepdims=True)
    return p, m_new, l_new, alpha


def _qk(q, k):
    return lax.dot_general(q, k, (((1,), (1,)), ((), ())), preferred_element_type=F32)


def _head_masks(dtype):
    return [(_lane_head((1, HW)) == h).astype(dtype) for h in range(N_HEADS)]


def _by_head(cols):
    lh = _lane_head((cols[0].shape[0], HW))
    out = cols[N_HEADS - 1]
    for h in range(N_HEADS - 2, -1, -1):
        out = jnp.where(lh == h, cols[h], out)
    return out


def _heads_init(q, qs_ref, m_ref, l_ref, acc_ref):
    for h, hm in enumerate(_head_masks(q.dtype)):
        qs_ref[h] = q * hm
    m_ref[...] = jnp.full(m_ref.shape, NEG, F32)
    l_ref[...] = jnp.zeros(l_ref.shape, F32)
    acc_ref[...] = jnp.zeros(acc_ref.shape, F32)


def _heads_chunk(qs_ref, k, v, logit_fn, m_ref, l_ref, acc_ref):
    ps, alphas = [], []
    for h in range(N_HEADS):
        s = logit_fn(h, _qk(qs_ref[h], k))
        p, m_new, l_new, alpha = _online_update(s, m_ref[h], l_ref[h])
        m_ref[h] = m_new
        l_ref[h] = l_new
        ps.append(p.astype(BF16))
        alphas.append(alpha)
    p_cat = jnp.concatenate(ps, axis=1)
    v_bd = jnp.concatenate([v * hm for hm in _head_masks(v.dtype)], axis=0)
    acc_ref[...] = _by_head(alphas) * acc_ref[...] + jnp.dot(p_cat, v_bd, preferred_element_type=F32)


def _heads_finish(l_ref, acc_ref):
    return acc_ref[...] / _by_head([l_ref[h] for h in range(N_HEADS)])


def _fox_body(q_ref, k_ref, v_ref, kc_ref, o_ref, qs_ref, m_ref, l_ref, acc_ref, *, tq, tk):
    i = pl.program_id(1)
    _heads_init(q_ref[0], qs_ref, m_ref, l_ref, acc_ref)
    qpos = i * tq + lax.broadcasted_iota(jnp.int32, (tq, tk), 0)
    kiota = lax.broadcasted_iota(jnp.int32, (tq, tk), 1)
    nk = (i * tq + tq + tk - 1) // tk

    def chunk(j, carry):
        off = pl.multiple_of(j * tk, tk)
        visible = (kiota + off) <= qpos
        logit = lambda h, s: jnp.where(visible, s - kc_ref[0, h:h + 1, pl.ds(off, tk)], NEG)
        _heads_chunk(qs_ref, k_ref[0, pl.ds(off, tk), :], v_ref[0, pl.ds(off, tk), :], logit,
                     m_ref, l_ref, acc_ref)
        return carry

    lax.fori_loop(0, nk, chunk, 0)
    o_ref[0] = _heads_finish(l_ref, acc_ref).astype(BF16)


def _fox_prompt(q, k, v, kcum, tq, tk):
    b, t, w = q.shape
    return pl.pallas_call(
        functools.partial(_fox_body, tq=tq, tk=tk),
        grid=(b, t // tq),
        in_specs=[pl.BlockSpec((1, tq, w), lambda bi, i: (bi, i, 0)),
                  pl.BlockSpec((1, t, w), lambda bi, i: (bi, 0, 0)),
                  pl.BlockSpec((1, t, w), lambda bi, i: (bi, 0, 0)),
                  pl.BlockSpec((1, SUB, t), lambda bi, i: (bi, 0, 0))],
        out_specs=pl.BlockSpec((1, tq, w), lambda bi, i: (bi, i, 0)),
        out_shape=jax.ShapeDtypeStruct((b, t, w), BF16),
        scratch_shapes=[pltpu.VMEM((N_HEADS, tq, w), BF16), pltpu.VMEM((N_HEADS, tq, 1), F32),
                        pltpu.VMEM((N_HEADS, tq, 1), F32), pltpu.VMEM((tq, w), F32)],
        compiler_params=_cparams("arbitrary", "arbitrary"),
        name="fox_prompt",
    )(q, k, v, kcum)


def _diff_finish(acc1, l1, acc2, l2, lam, gain, out_scale):
    o = acc1 / l1 - lam * (acc2 / l2)
    return _rms(o, gain) * out_scale


def _lambda(lq1, lk1, lq2, lk2, lam_init):
    return (jnp.exp(jnp.sum(lq1 * lk1, axis=-1, keepdims=True))
            - jnp.exp(jnp.sum(lq2 * lk2, axis=-1, keepdims=True)) + lam_init)


def _diff_body(q_ref, k_ref, v_ref, lq1_ref, lk1_ref, lq2_ref, lk2_ref, sg_ref, o_ref,
               qs_ref, m_ref, l_ref, acc_ref, *, tq, tk, lam_init):
    i = pl.program_id(1)
    nmap = 2 * N_HEADS
    for half in range(2):
        q = q_ref[0, :, half * HW:(half + 1) * HW]
        lh = _lane_head(q.shape)
        for g in range(N_HEADS):
            qs_ref[half * N_HEADS + g] = jnp.where(lh == g, q, jnp.zeros_like(q))
    m_ref[...] = jnp.full(m_ref.shape, NEG, F32)
    l_ref[...] = jnp.zeros(l_ref.shape, F32)
    acc_ref[...] = jnp.zeros(acc_ref.shape, F32)
    qpos = i * tq + lax.broadcasted_iota(jnp.int32, (tq, tk), 0)
    kiota = lax.broadcasted_iota(jnp.int32, (tq, tk), 1)
    nk = (i * tq + tq + tk - 1) // tk

    def chunk(j, carry):
        off = pl.multiple_of(j * tk, tk)
        visible = (kiota + off) <= qpos
        for mp in range(nmap):
            half, head = mp // N_HEADS, mp // 2
            k = k_ref[0, pl.ds(off, tk), half * HW:(half + 1) * HW]
            v = v_ref[0, pl.ds(off, tk), head * 2 * HEAD_DIM:(head + 1) * 2 * HEAD_DIM]
            s = jnp.where(visible, _qk(qs_ref[mp], k), NEG)
            p, m_new, l_new, alpha = _online_update(s, m_ref[mp], l_ref[mp])
            m_ref[mp] = m_new
            l_ref[mp] = l_new
            acc_ref[mp] = alpha * acc_ref[mp] + jnp.dot(p.astype(BF16), v, preferred_element_type=F32)
        return carry

    lax.fori_loop(0, nk, chunk, 0)
    lam = _lambda(lq1_ref[...], lk1_ref[...], lq2_ref[...], lk2_ref[...], lam_init)
    for head in range(N_HEADS):
        o = _diff_finish(acc_ref[2 * head], l_ref[2 * head], acc_ref[2 * head + 1], l_ref[2 * head + 1],
                         lam, sg_ref[...], 1.0 - lam_init)
        o_ref[0, :, head * 2 * HEAD_DIM:(head + 1) * 2 * HEAD_DIM] = o.astype(BF16)


def _diff_prompt(q, k, v, p, lam_init, tq, tk):
    b, t, w = q.shape
    nmap = 2 * N_HEADS
    vec = _const_spec((1, HEAD_DIM))
    return pl.pallas_call(
        functools.partial(_diff_body, tq=tq, tk=tk, lam_init=lam_init),
        grid=(b, t // tq),
        in_specs=[pl.BlockSpec((1, tq, w), lambda bi, i: (bi, i, 0)),
                  pl.BlockSpec((1, t, w), lambda bi, i: (bi, 0, 0)),
                  pl.BlockSpec((1, t, w), lambda bi, i: (bi, 0, 0)),
                  vec, vec, vec, vec, _const_spec((1, 2 * HEAD_DIM))],
        out_specs=pl.BlockSpec((1, tq, w), lambda bi, i: (bi, i, 0)),
        out_shape=jax.ShapeDtypeStruct((b, t, w), BF16),
        scratch_shapes=[pltpu.VMEM((nmap, tq, HW), BF16), pltpu.VMEM((nmap, tq, 1), F32),
                        pltpu.VMEM((nmap, tq, 1), F32), pltpu.VMEM((nmap, tq, 2 * HEAD_DIM), F32)],
        compiler_params=_cparams("arbitrary", "arbitrary"),
        name="diff_prompt",
    )(q, k, v, p["lq1"], p["lk1"], p["lq2"], p["lk2"], p["c_subln"])


INT_MIN = -2 ** 31
KEY_NEG_INF = int(0xFF800000 - 2 ** 32) ^ 0x7FFFFFFF


def _sort_key(score):
    bits = lax.bitcast_convert_type(score, jnp.int32)
    return jnp.where(bits < 0, bits ^ jnp.int32(0x7FFFFFFF), bits)


COUNT_UNROLL = 4
COUNT_W = COUNT_UNROLL * 128


def _count_keys(keys_ref, n_iter, hit, operands):
    nq = keys_ref.shape[1]
    pos = lax.broadcasted_iota(jnp.int32, (128, nq), 0)

    def chunk(c, acc):
        for u in range(COUNT_UNROLL):
            off = pl.multiple_of(c * COUNT_W + u * 128, 128)
            h = hit(keys_ref[pl.ds(off, 128), :], pos + off, *operands)
            acc = acc + sum(h[r:r + SUB] for r in range(0, 128, SUB))
        return acc

    acc = lax.fori_loop(0, n_iter, chunk, jnp.zeros((SUB, nq), F32))
    return jnp.sum(acc, axis=0, keepdims=True)


def _topk_select(keys_ref, n_iter, topk, pos_bits):
    nq = keys_ref.shape[1]
    kf = float(topk)
    one = lambda m: jnp.where(m, 1.0, 0.0)
    count = functools.partial(_count_keys, keys_ref, n_iter)
    cnt0 = count(lambda kk, pos: one(kk >= 0), [])
    thr0 = jnp.where(cnt0 >= kf, jnp.int32(0), jnp.int32(INT_MIN))

    def bit_step(t, thr):
        cand = thr | jnp.left_shift(jnp.int32(1), 30 - t)
        cnt = count(lambda kk, pos, cd: one(kk >= cd), [cand])
        return jnp.where(cnt >= kf, cand, thr)

    thr = lax.fori_loop(0, 31, bit_step, thr0)
    n_gt = count(lambda kk, pos, th: one(kk > th), [thr])
    n_ge = count(lambda kk, pos, th: one(kk >= th), [thr])
    need = kf - n_gt
    excess = jnp.where(n_ge > kf, one(thr > KEY_NEG_INF), 0.0)
    all_pos = jnp.full((1, nq), 2 ** pos_bits - 1, jnp.int32)

    def tie_search():
        def pos_step(t, x):
            cand = x | jnp.left_shift(jnp.int32(1), pos_bits - 1 - t)
            cnt = count(lambda kk, pos, th, cd: jnp.where(kk == th, one(pos < cd), 0.0), [thr, cand])
            return jnp.where(cnt < need, cand, x)
        return lax.fori_loop(0, pos_bits, pos_step, jnp.zeros((1, nq), jnp.int32))

    xpos = lax.cond(jnp.max(excess) > 0.0, tie_search, lambda: all_pos)
    return thr, xpos


def _select_bias(kk, pos, thr, xpos):
    tie = jnp.where(pos <= xpos, 0.0, NEG)
    sel = jnp.where(kk > thr, 0.0, jnp.where(kk == thr, tie, NEG))
    return jnp.where(kk > KEY_NEG_INF, sel, NEG)


def _dsa_body(q_ref, iq_ref, iw_ref, k_ref, v_ref, ik_ref, o_ref,
              qs_ref, iqs_ref, keys_ref, m_ref, l_ref, acc_ref, *, tq, tk, topk, pos_bits):
    i = pl.program_id(1)
    iq = iq_ref[0]
    for h, hm in enumerate(_head_masks(iq.dtype)):
        iqs_ref[h] = iq * hm
    qpos = i * tq + lax.broadcasted_iota(jnp.int32, (tk, tq), 1)
    kiota = lax.broadcasted_iota(jnp.int32, (tk, tq), 0)
    nk = (i * tq + tq + tk - 1) // tk
    iw = iw_ref[0]

    def score_chunk(j, carry):
        off = pl.multiple_of(j * tk, tk)
        ik = ik_ref[0, pl.ds(off, tk), :]
        score = jnp.zeros((tk, tq), F32)
        for h in range(N_HEADS):
            score = score + iw[h:h + 1, :] * jnp.maximum(_qk(ik, iqs_ref[h]), 0.0)
        score = jnp.where((kiota + off) <= qpos, score, -jnp.inf)
        keys_ref[pl.ds(off, tk), :] = _sort_key(score)
        return carry

    lax.fori_loop(0, nk, score_chunk, 0)
    thr, xpos = _topk_select(keys_ref, nk * (tk // COUNT_W), topk, pos_bits)

    _heads_init(q_ref[0], qs_ref, m_ref, l_ref, acc_ref)

    def chunk(j, carry):
        off = pl.multiple_of(j * tk, tk)
        bias = _select_bias(keys_ref[pl.ds(off, tk), :], kiota + off, thr, xpos).T
        _heads_chunk(qs_ref, k_ref[0, pl.ds(off, tk), :], v_ref[0, pl.ds(off, tk), :],
                     lambda h, s: s + bias, m_ref, l_ref, acc_ref)
        return carry

    lax.fori_loop(0, nk, chunk, 0)
    o_ref[0] = _heads_finish(l_ref, acc_ref).astype(BF16)


def _dsa_prompt(q, iq, iw_t, k, v, ik, tq, tk):
    b, t, w = q.shape
    topk = min(TOPK_MAX, t // 4)
    assert tk >= topk and tk % COUNT_W == 0
    return pl.pallas_call(
        functools.partial(_dsa_body, tq=tq, tk=tk, topk=topk, pos_bits=max(t - 1, 1).bit_length()),
        grid=(b, t // tq),
        in_specs=[pl.BlockSpec((1, tq, w), lambda bi, i: (bi, i, 0)),
                  pl.BlockSpec((1, tq, w), lambda bi, i: (bi, i, 0)),
                  pl.BlockSpec((1, SUB, tq), lambda bi, i: (bi, 0, i)),
                  pl.BlockSpec((1, t, w), lambda bi, i: (bi, 0, 0)),
                  pl.BlockSpec((1, t, w), lambda bi, i: (bi, 0, 0)),
                  pl.BlockSpec((1, t, w), lambda bi, i: (bi, 0, 0))],
        out_specs=pl.BlockSpec((1, tq, w), lambda bi, i: (bi, i, 0)),
        out_shape=jax.ShapeDtypeStruct((b, t, w), BF16),
        scratch_shapes=[pltpu.VMEM((N_HEADS, tq, w), BF16), pltpu.VMEM((N_HEADS, tq, w), BF16),
                        pltpu.VMEM((t, tq), jnp.int32),
                        pltpu.VMEM((N_HEADS, tq, 1), F32), pltpu.VMEM((N_HEADS, tq, 1), F32),
                        pltpu.VMEM((tq, w), F32)],
        compiler_params=_cparams("arbitrary", "arbitrary"),
        name="dsa_prompt",
    )(q, iq, iw_t, k, v, ik)


def _post_body(x_ref, oa_ref, ob_ref, oc_ref, g_ref, wgate_ref, bgate_ref, woa_ref, wob_ref, woc_ref, wout_ref,
               o_ref, acc_ref, *, mc):
    x = x_ref[...]
    d = x.shape[1]
    h = _rms(x, g_ref[...]).astype(BF16)
    oa, ob, oc = oa_ref[...], ob_ref[...], oc_ref[...]
    for c in range(d // mc):
        merged = None
        for br, (o, w_ref) in enumerate(((oa, woa_ref), (ob, wob_ref), (oc, woc_ref))):
            sl = slice(br * d + c * mc, br * d + (c + 1) * mc)
            gate = _sigmoid(jnp.dot(h, wgate_ref[:, sl], preferred_element_type=F32) + bgate_ref[:, sl])
            term = gate * jnp.dot(o, w_ref[:, c * mc:(c + 1) * mc], preferred_element_type=F32)
            merged = term if merged is None else merged + term
        part = jnp.dot(merged.astype(BF16), wout_ref[c * mc:(c + 1) * mc, :], preferred_element_type=F32)
        if c == 0:
            acc_ref[...] = part
        else:
            acc_ref[...] += part
    o_ref[...] = x + acc_ref[...]


def _post(x, oa, ob, oc, p, tm):
    n, d = x.shape
    row = lambda w: pl.BlockSpec((tm, w), lambda i: (i, 0))
    return pl.pallas_call(
        functools.partial(_post_body, mc=256),
        grid=(n // tm,),
        in_specs=[row(d), row(HW), row(HW), row(CW), _const_spec((1, d)),
                  _const_spec((d, 3 * d)), _const_spec((1, 3 * d)),
                  _const_spec((HW, d)), _const_spec((HW, d)), _const_spec((CW, d)), _const_spec((d, d))],
        out_specs=row(d),
        out_shape=jax.ShapeDtypeStruct((n, d), F32),
        scratch_shapes=[pltpu.VMEM((tm, d), F32)],
        compiler_params=_cparams("arbitrary"),
        name="post",
    )(x, oa, ob, oc, p["norm_mix"], p["w_gate"], p["b_gate"], p["w_o_a"], p["w_o_b"], p["w_o_c"], p["w_out"])


def _expand_rows(x, reps):
    return jnp.concatenate([jnp.broadcast_to(x[g:g + 1], (SUB, x.shape[1])) for g in range(reps)], axis=0)


def _new_visible():
    r = lax.broadcasted_iota(jnp.int32, (SUB, PAGE_SIZE), 0)
    c = lax.broadcasted_iota(jnp.int32, (SUB, PAGE_SIZE), 1)
    return c <= r


def _sscore_body(pt_ref, iq_ref, iw_ref, *refs, npg):
    pages, new_ref, keys_ref = refs[:npg], refs[npg], refs[npg + 1]

    def scores(kt):
        sc = jnp.maximum(jnp.dot(iq_ref[0], kt, preferred_element_type=F32), 0.0) * iw_ref[0]
        return sc[0:SUB] + sc[SUB:2 * SUB] + sc[2 * SUB:3 * SUB] + sc[3 * SUB:4 * SUB]

    for j in range(npg):
        keys_ref[0, :, j * PAGE_SIZE:(j + 1) * PAGE_SIZE] = _sort_key(scores(pages[j][...].astype(BF16)))
    s_new = jnp.where(_new_visible(), scores(new_ref[0]), -jnp.inf)
    keys_ref[0, :, npg * PAGE_SIZE:(npg + 1) * PAGE_SIZE] = _sort_key(s_new)
    pad = keys_ref.shape[2] - (npg + 1) * PAGE_SIZE
    if pad:
        keys_ref[0, :, (npg + 1) * PAGE_SIZE:] = jnp.full((SUB, pad), KEY_NEG_INF, jnp.int32)


def _sample_scores(pt, iq_rows, iw_rows, idx_t, new_idx_t, layer, ncols):
    nb = iq_rows.shape[0]
    npg = pt.shape[0] // nb
    page = lambda j: pl.BlockSpec((None, None, HEAD_DIM, PAGE_SIZE),
                                  lambda b, pt: (layer, pt[b * npg + j], 0, 0))
    return pl.pallas_call(
        functools.partial(_sscore_body, npg=npg),
        grid_spec=pltpu.PrefetchScalarGridSpec(
            num_scalar_prefetch=1,
            grid=(nb,),
            in_specs=[pl.BlockSpec((1, N_HEADS * SUB, HEAD_DIM), lambda b, pt: (b, 0, 0)),
                      pl.BlockSpec((1, N_HEADS * SUB, 1), lambda b, pt: (b, 0, 0))]
                     + [page(j) for j in range(npg)]
                     + [pl.BlockSpec((1, HEAD_DIM, PAGE_SIZE), lambda b, pt: (b, 0, 0))],
            out_specs=pl.BlockSpec((1, SUB, ncols), lambda b, pt: (b, 0, 0))),
        out_shape=jax.ShapeDtypeStruct((nb, SUB, ncols), jnp.int32),
        compiler_params=_cparams("arbitrary"),
        name="sample_scores",
    )(pt, iq_rows, iw_rows, *([idx_t] * npg), new_idx_t)


def _sselect_body(keys_ref, bias_ref, *, topk, pos_bits):
    nkeys, nq = keys_ref.shape
    thr, xpos = _topk_select(keys_ref, nkeys // COUNT_W, topk, pos_bits)
    pos = lax.broadcasted_iota(jnp.int32, (128, nq), 0)
    for g in range(nkeys // 128):
        sl = slice(g * 128, (g + 1) * 128)
        bias_ref[sl, :] = _select_bias(keys_ref[sl, :], pos + g * 128, thr, xpos)


def _sample_select(keys_t, topk):
    nkeys, nq = keys_t.shape
    tl = min(256, nq)
    return pl.pallas_call(
        functools.partial(_sselect_body, topk=topk, pos_bits=max(nkeys - 1, 1).bit_length()),
        grid=(nq // tl,),
        in_specs=[pl.BlockSpec((nkeys, tl), lambda i: (0, i))],
        out_specs=pl.BlockSpec((nkeys, tl), lambda i: (0, i)),
        out_shape=jax.ShapeDtypeStruct((nkeys, nq), F32),
        compiler_params=_cparams("arbitrary"),
        name="sample_select",
    )(keys_t)


def _sattn_body(pt_ref, aq_ref, bq_ref, cq_ref, bias_ref, biasn_ref, *refs, pg, n_steps, lam_init):
    it = iter(refs)
    take = lambda n: [next(it) for _ in range(n)]
    ak_refs, av_refs, lf_refs, bk_refs, bv_refs, ck_refs, cv_refs = (take(pg) for _ in range(7))
    akn_ref, avn_ref, lfn_ref, bkn_ref, bvn_ref, ckn_ref, cvn_ref = take(7)
    lq1_ref, lk1_ref, lq2_ref, lk2_ref, sg_ref = take(5)
    oa_ref, ob_ref, oc_ref = take(3)
    ma_ref, la_ref, acca_ref, mb_ref, lb_ref, accb_ref, mc_ref, lc_ref, accc_ref, ctot_ref = take(10)
    p = pl.program_id(1)
    nmap = 2 * N_HEADS

    @pl.when(p == 0)
    def _():
        for m_ref, l_ref, acc_ref in ((ma_ref, la_ref, acca_ref), (mb_ref, lb_ref, accb_ref),
                                      (mc_ref, lc_ref, accc_ref)):
            m_ref[...] = jnp.full(m_ref.shape, NEG, F32)
            l_ref[...] = jnp.zeros(l_ref.shape, F32)
            acc_ref[...] = jnp.zeros(acc_ref.shape, F32)
        ctot_ref[...] = jnp.zeros(ctot_ref.shape, F32)

    r = lax.broadcasted_iota(jnp.int32, (PAGE_SIZE, PAGE_SIZE), 0)
    c = lax.broadcasted_iota(jnp.int32, (PAGE_SIZE, PAGE_SIZE), 1)
    tri = jnp.where(r <= c, 1.0, 0.0).astype(F32)
    lanes = lambda xs: jnp.concatenate(xs, axis=1) if len(xs) > 1 else xs[0]
    page_cols = lambda x, j: x[:, j * PAGE_SIZE:(j + 1) * PAGE_SIZE]

    def step(akt, avt, lfs, bkt, bvt, ckt, cvh, bias, vis):
        n = len(akt)
        pad = jnp.zeros((SUB - N_HEADS, PAGE_SIZE), F32)
        in_page = [jnp.dot(jnp.concatenate([lf, pad], axis=0), tri, preferred_element_type=F32,
                           precision=lax.Precision.HIGHEST) for lf in lfs]
        tot = ctot_ref[...]
        cums = []
        for j in range(n):
            cum = in_page[j] + tot
            tot = cum[:, PAGE_SIZE - 1:PAGE_SIZE]
            cums.append(_expand_rows(cum, N_HEADS))
        ctot_ref[...] = tot
        dots = lambda q, kts: lanes([jnp.dot(q, kt, preferred_element_type=F32) for kt in kts])
        s = dots(aq_ref[0], akt) - lanes(cums)
        if vis is not None:
            s = jnp.where(jnp.concatenate([vis] * N_HEADS, axis=0), s, NEG)
        pa, m_new, l_new, alpha = _online_update(s, ma_ref[...], la_ref[...])
        ma_ref[...] = m_new
        la_ref[...] = l_new
        pa = pa.astype(BF16)
        acca_ref[...] = alpha * acca_ref[...] + sum(_qk(page_cols(pa, j), avt[j]) for j in range(n))
        s = dots(bq_ref[0], bkt) + jnp.concatenate([bias] * N_HEADS, axis=0)
        pb, m_new, l_new, alpha = _online_update(s, mb_ref[...], lb_ref[...])
        mb_ref[...] = m_new
        lb_ref[...] = l_new
        pb = pb.astype(BF16)
        accb_ref[...] = alpha * accb_ref[...] + sum(_qk(page_cols(pb, j), bvt[j]) for j in range(n))
        s = dots(cq_ref[0], ckt)
        if vis is not None:
            s = jnp.where(jnp.concatenate([vis] * nmap, axis=0), s, NEG)
        pc, m_new, l_new, alpha = _online_update(s, mc_ref[...], lc_ref[...])
        mc_ref[...] = m_new
        lc_ref[...] = l_new
        pc = pc.astype(BF16)
        for head in range(N_HEADS):
            rows = slice(head * 2 * SUB, (head + 1) * 2 * SUB)
            pv = sum(jnp.dot(page_cols(pc[rows], j), cvh[j][head], preferred_element_type=F32) for j in range(n))
            accc_ref[rows, :] = alpha[rows] * accc_ref[rows, :] + pv

    bf = lambda refs_: [rf[...].astype(BF16) for rf in refs_]
    step(bf(ak_refs), bf(av_refs), [rf[...] for rf in lf_refs], bf(bk_refs), bf(bv_refs), bf(ck_refs),
         [[rf[pl.ds(head, PAGE_SIZE, stride=N_HEADS), :].astype(BF16) for head in range(N_HEADS)]
          for rf in cv_refs], bias_ref[0], None)

    @pl.when(p == n_steps - 1)
    def _():
        cvn = cvn_ref[0]
        step([akn_ref[0]], [avn_ref[0]], [lfn_ref[0]], [bkn_ref[0]], [bvn_ref[0]], [ckn_ref[0]],
             [[cvn[:, head * 2 * HEAD_DIM:(head + 1) * 2 * HEAD_DIM] for head in range(N_HEADS)]],
             biasn_ref[0], _new_visible())
        lho = _lane_head((SUB, HW))
        for acc_ref, l_ref, o_ref in ((acca_ref, la_ref, oa_ref), (accb_ref, lb_ref, ob_ref)):
            o = acc_ref[...] / l_ref[...]
            out = jnp.zeros((SUB, HW), F32)
            for h in range(N_HEADS):
                out = jnp.where(lho == h, o[h * SUB:(h + 1) * SUB], out)
            o_ref[0] = out.astype(BF16)
        lam = _lambda(lq1_ref[...], lk1_ref[...], lq2_ref[...], lk2_ref[...], lam_init)
        accc = accc_ref[...]
        lc = lc_ref[...]
        for head in range(N_HEADS):
            r1 = slice(head * 2 * SUB, head * 2 * SUB + SUB)
            r2 = slice(head * 2 * SUB + SUB, (head + 1) * 2 * SUB)
            o = _diff_finish(accc[r1], lc[r1], accc[r2], lc[r2], lam, sg_ref[...], 1.0 - lam_init)
            oc_ref[0, :, head * 2 * HEAD_DIM:(head + 1) * 2 * HEAD_DIM] = o.astype(BF16)


def _sample_attention(pt, q_rows, bias, caches, new_rows, p, layer, lam_init, pg):
    aq, bq, cq = q_rows
    nb = aq.shape[0]
    npg = pt.shape[0] // nb
    n_steps = npg // pg
    nmap = 2 * N_HEADS

    def pages(shape):
        nd = len(shape)
        return [pl.BlockSpec((None, None) + shape,
                             lambda b, p, pt, j=j: (layer, pt[b * npg + p * pg + j]) + (0,) * nd)
                for j in range(pg)]

    per_seq = lambda shape: pl.BlockSpec((1,) + shape, lambda b, p, pt: (b,) + (0,) * len(shape))
    const = lambda shape: pl.BlockSpec(shape, lambda b, p, pt: (0,) * len(shape))
    in_specs = ([per_seq((N_HEADS * SUB, HW)), per_seq((N_HEADS * SUB, HW)), per_seq((nmap * SUB, CW)),
                 pl.BlockSpec((1, SUB, pg * PAGE_SIZE), lambda b, p, pt: (b, 0, p)),
                 pl.BlockSpec((1, SUB, PAGE_SIZE), lambda b, p, pt: (b, 0, npg))]
                + pages((HW, PAGE_SIZE)) + pages((HW, PAGE_SIZE)) + pages((N_HEADS, PAGE_SIZE))
                + pages((HW, PAGE_SIZE)) + pages((HW, PAGE_SIZE)) + pages((CW, PAGE_SIZE))
                + pages((PAGE_SIZE * N_HEADS, 2 * HEAD_DIM))
                + [per_seq((HW, PAGE_SIZE)), per_seq((HW, PAGE_SIZE)), per_seq((N_HEADS, PAGE_SIZE)),
                   per_seq((HW, PAGE_SIZE)), per_seq((HW, PAGE_SIZE)), per_seq((CW, PAGE_SIZE)),
                   per_seq((PAGE_SIZE, CW)),
                   const((1, HEAD_DIM)), const((1, HEAD_DIM)), const((1, HEAD_DIM)), const((1, HEAD_DIM)),
                   const((1, 2 * HEAD_DIM))])
    cache_args = [c for c in caches for _ in range(pg)]
    out_specs = [per_seq((SUB, HW)), per_seq((SUB, HW)), per_seq((SUB, CW))]
    scratch = [pltpu.VMEM((N_HEADS * SUB, 1), F32), pltpu.VMEM((N_HEADS * SUB, 1), F32),
               pltpu.VMEM((N_HEADS * SUB, HW), F32),
               pltpu.VMEM((N_HEADS * SUB, 1), F32), pltpu.VMEM((N_HEADS * SUB, 1), F32),
               pltpu.VMEM((N_HEADS * SUB, HW), F32),
               pltpu.VMEM((nmap * SUB, 1), F32), pltpu.VMEM((nmap * SUB, 1), F32),
               pltpu.VMEM((nmap * SUB, 2 * HEAD_DIM), F32),
               pltpu.VMEM((SUB, 1), F32)]
    return pl.pallas_call(
        functools.partial(_sattn_body, pg=pg, n_steps=n_steps, lam_init=lam_init),
        grid_spec=pltpu.PrefetchScalarGridSpec(
            num_scalar_prefetch=1, grid=(nb, n_steps),
            in_specs=in_specs, out_specs=out_specs, scratch_shapes=scratch),
        out_shape=[jax.ShapeDtypeStruct((nb, SUB, HW), BF16), jax.ShapeDtypeStruct((nb, SUB, HW), BF16),
                   jax.ShapeDtypeStruct((nb, SUB, CW), BF16)],
        compiler_params=_cparams("arbitrary", "arbitrary"),
        name="sample_attention",
    )(pt, aq, bq, cq, bias, bias, *cache_args, *new_rows, p["lq1"], p["lk1"], p["lq2"], p["lk2"], p["c_subln"])


def _rope_tables(pos):
    half = ROPE_DIM // 2
    inv_freq = ROPE_THETA ** (-jnp.arange(half, dtype=F32) / half)
    ang = pos.astype(F32)[:, None] * inv_freq[None, :]
    cos, sin = jnp.cos(ang), jnp.sin(ang)
    n = pos.shape[0]
    one = jnp.ones((n, HEAD_DIM - ROPE_DIM), F32)
    zero = jnp.zeros((n, HEAD_DIM - ROPE_DIM), F32)
    z8 = jnp.zeros((n, half), F32)
    c = jnp.concatenate([cos, cos, one], axis=1)
    sa = jnp.concatenate([z8, sin, zero], axis=1)
    sb = jnp.concatenate([-sin, z8, zero], axis=1)
    return tuple(jnp.tile(t, (1, 2)) for t in (c, sa, sb))


def _masked_query_rows(q, n_seq, n_new, n_groups):
    w = q.shape[1]
    q = q.reshape(n_seq, 1, n_new, w)
    q = jnp.pad(q, ((0, 0), (0, 0), (0, SUB - n_new), (0, 0)))
    grp = (jnp.arange(w) // HEAD_DIM)[None, :] == jnp.arange(n_groups)[:, None]
    q = jnp.where(grp[None, :, None, :], q, jnp.zeros((), q.dtype))
    return q.reshape(n_seq, n_groups * SUB, w)


def _pad_keys(x, n_seq, n_new):
    x = x.reshape(n_seq, n_new, x.shape[1])
    return jnp.pad(x, ((0, 0), (0, PAGE_SIZE - n_new), (0, 0)))


def _pad_keys_t(x, n_seq, n_new):
    x = jnp.swapaxes(x.reshape(n_seq, n_new, x.shape[1]), 1, 2)
    return jnp.pad(x, ((0, 0), (0, 0), (0, PAGE_SIZE - n_new)))


def _layer_params(l, a):
    bf = lambda x: x.astype(BF16)
    row = lambda x: x.reshape(1, -1)
    d = a["norm_mix"].shape[1]
    fox_bias = jnp.zeros((1, MISC_W), F32).at[0, :N_HEADS].set(a["fox_bias_f"][l])
    blk = jnp.arange(HW) // HEAD_DIM
    bd = jnp.where(blk[:, None] == blk[None, :], 1.0 / HEAD_DIM, 0.0).astype(BF16)
    return {
        "norm_ffn1": row(a["norm_ffn1"][l]), "norm_mix": row(a["norm_mix"][l]), "norm_ffn2": row(a["norm_ffn2"][l]),
        "ffn1": (bf(a["ffn1_w_gate"][l]), bf(a["ffn1_w_up"][l]), bf(a["ffn1_w_down"][l])),
        "ffn2": (bf(a["ffn2_w_gate"][l]), bf(a["ffn2_w_up"][l]), bf(a["ffn2_w_down"][l])),
        "w_in": bf(_pack_w_in(a["w_in"][l])), "fox_bias": fox_bias, "bd": bd,
        "gaq": row(jnp.tile(a["a_q_norm"][l], N_HEADS)), "gak": row(jnp.tile(a["a_k_norm"][l], N_HEADS)),
        "gbq": row(jnp.tile(a["b_q_norm"][l], N_HEADS)), "gbk": row(jnp.tile(a["b_k_norm"][l], N_HEADS)),
        "gcq": row(jnp.tile(a["c_q_norm"][l].reshape(-1), N_HEADS)),
        "gck": row(jnp.tile(a["c_k_norm"][l].reshape(-1), N_HEADS)),
        "lq1": row(a["c_lambda_q1"][l]), "lk1": row(a["c_lambda_k1"][l]),
        "lq2": row(a["c_lambda_q2"][l]), "lk2": row(a["c_lambda_k2"][l]),
        "c_subln": row(a["c_subln"][l]),
        "w_gate": bf(a["w_gate"][l]), "b_gate": row(a["b_gate"][l]),
        "w_o_a": bf(a["w_o_a"][l]), "w_o_b": bf(a["w_o_b"][l]), "w_o_c": bf(a["w_o_c"][l]),
        "w_out": bf(a["w_out"][l]),
    }


_INPROJ_NAMES = ("aq", "akf", "akb", "avf", "avb", "bq", "bkf", "bkb", "bvf", "bvb", "biq", "bif", "bib",
                 "cq", "ckf", "ckb", "cvf", "cvb", "misc", "cum")


def kernel(x_prompt, x_sample, cache_a_k, cache_a_v, cache_a_logf, cache_b_k, cache_b_v, cache_b_idx, cache_c_k, cache_c_v, page_table, norm_ffn1, ffn1_w_gate, ffn1_w_up, ffn1_w_down, norm_mix, w_in, fox_bias_f, a_q_norm, a_k_norm, b_q_norm, b_k_norm, c_q_norm, c_k_norm, c_lambda_q1, c_lambda_k1, c_lambda_q2, c_lambda_k2, c_subln, w_o_a, w_o_b, w_o_c, w_gate, b_gate, w_out, norm_ffn2, ffn2_w_gate, ffn2_w_up, ffn2_w_down):
    a = dict(norm_ffn1=norm_ffn1, ffn1_w_gate=ffn1_w_gate, ffn1_w_up=ffn1_w_up, ffn1_w_down=ffn1_w_down,
             norm_mix=norm_mix, w_in=w_in, fox_bias_f=fox_bias_f, a_q_norm=a_q_norm, a_k_norm=a_k_norm,
             b_q_norm=b_q_norm, b_k_norm=b_k_norm, c_q_norm=c_q_norm, c_k_norm=c_k_norm,
             c_lambda_q1=c_lambda_q1, c_lambda_k1=c_lambda_k1, c_lambda_q2=c_lambda_q2, c_lambda_k2=c_lambda_k2,
             c_subln=c_subln, w_o_a=w_o_a, w_o_b=w_o_b, w_o_c=w_o_c, w_gate=w_gate, b_gate=b_gate, w_out=w_out,
             norm_ffn2=norm_ffn2, ffn2_w_gate=ffn2_w_gate, ffn2_w_up=ffn2_w_up, ffn2_w_down=ffn2_w_down)
    depth = w_in.shape[0]
    bsz, seq, d = x_prompt.shape
    nb, n_new, _ = x_sample.shape
    n_pool = cache_a_k.shape[1]
    npg = page_table.shape[1]
    past_len = npg * PAGE_SIZE
    np_tok, ns_tok = bsz * seq, nb * n_new
    tm_p = min(512, seq)
    tm_s = min(512, ns_tok)
    tq = min(256, seq)
    tk = min(512, seq)

    assert n_new <= SUB and np_tok % tm_p == 0 and ns_tok % tm_s == 0 and tm_s % n_new == 0
    keys_last = lambda c, w: jnp.moveaxis(c, 2, -1).reshape(depth, n_pool, w, PAGE_SIZE)
    caches = (keys_last(cache_a_k, HW), keys_last(cache_a_v, HW), keys_last(cache_a_logf, N_HEADS),
              keys_last(cache_b_k, HW), keys_last(cache_b_v, HW), keys_last(cache_c_k, CW),
              cache_c_v.reshape(depth, n_pool, PAGE_SIZE * N_HEADS, 2 * HEAD_DIM))
    idx_t = keys_last(cache_b_idx, HEAD_DIM)
    pt = page_table.reshape(-1).astype(jnp.int32)
    pg = 4 if npg % 4 == 0 else (2 if npg % 2 == 0 else 1)
    tk_total = past_len + n_new
    ncols = -(-(npg + 1) * PAGE_SIZE // COUNT_W) * COUNT_W
    topk_s = min(TOPK_MAX, tk_total // 4)

    tabs_p = _rope_tables(jnp.arange(seq, dtype=jnp.int32))
    pos_s = past_len + jnp.arange(n_new, dtype=jnp.int32)
    tabs_s = _rope_tables(jnp.tile(pos_s, tm_s // n_new))

    xp = x_prompt.reshape(np_tok, d)
    xs = x_sample.reshape(ns_tok, d)
    rows_p, rows_s = [], []
    for l in range(depth):
        p = _layer_params(l, a)
        lam_init = 0.8 - 0.6 * math.exp(-0.3 * l)

        xp = _ffn(xp, p["norm_ffn1"], *p["ffn1"], tm_p)
        pr = dict(zip(_INPROJ_NAMES, _inproj(xp, p, tabs_p, tm_p, seq)))
        b3 = lambda z: z.reshape(bsz, seq, z.shape[1])
        kcum = jnp.swapaxes(b3(pr["cum"])[:, :, :SUB], 1, 2)
        oa = _fox_prompt(b3(pr["aq"]), b3(pr["akb"]), b3(pr["avb"]), kcum, tq, tk)
        iw_t = jnp.swapaxes(b3(pr["misc"])[:, :, N_HEADS:N_HEADS + SUB], 1, 2)
        ob = _dsa_prompt(b3(pr["bq"]), b3(pr["biq"]), iw_t, b3(pr["bkb"]), b3(pr["bvb"]), b3(pr["bib"]), tq, tk)
        oc = _diff_prompt(b3(pr["cq"]), b3(pr["ckb"]), b3(pr["cvb"]), p, lam_init, tq, tk)
        xp = _post(xp, oa.reshape(np_tok, HW), ob.reshape(np_tok, HW), oc.reshape(np_tok, CW), p, tm_p)
        xp = _ffn(xp, p["norm_ffn2"], *p["ffn2"], tm_p)
        rows_p.append(pr)

        xs = _ffn(xs, p["norm_ffn1"], *p["ffn1"], tm_s)
        sr = dict(zip(_INPROJ_NAMES, _inproj(xs, p, tabs_s, tm_s, ns_tok)))
        iq = jnp.swapaxes(sr["biq"].reshape(nb, n_new, N_HEADS, HEAD_DIM), 1, 2)
        iq_rows = jnp.pad(iq, ((0, 0), (0, 0), (0, SUB - n_new), (0, 0))).reshape(nb, N_HEADS * SUB, HEAD_DIM)
        iw = sr["misc"][:, N_HEADS:2 * N_HEADS].reshape(nb, n_new, N_HEADS)
        iw = jnp.pad(jnp.swapaxes(iw, 1, 2), ((0, 0), (0, 0), (0, SUB - n_new)))
        iw_rows = iw.reshape(nb, N_HEADS * SUB, 1)
        new_idx_t = _pad_keys_t(sr["bib"][:, :HEAD_DIM], nb, n_new)
        keys = _sample_scores(pt, iq_rows, iw_rows, idx_t, new_idx_t, l, ncols)
        bias = _sample_select(keys.reshape(nb * SUB, ncols).T, topk_s).T.reshape(nb, SUB, ncols)
        q_rows = (_masked_query_rows(sr["aq"], nb, n_new, N_HEADS),
                  _masked_query_rows(sr["bq"], nb, n_new, N_HEADS),
                  _masked_query_rows(sr["cq"], nb, n_new, 2 * N_HEADS))
        new_rows = (_pad_keys_t(sr["akb"], nb, n_new), _pad_keys_t(sr["avb"], nb, n_new),
                    _pad_keys_t(sr["misc"][:, :N_HEADS], nb, n_new),
                    _pad_keys_t(sr["bkb"], nb, n_new), _pad_keys_t(sr["bvb"], nb, n_new),
                    _pad_keys_t(sr["ckb"], nb, n_new), _pad_keys(sr["cvb"], nb, n_new))
        soa, sob, soc = _sample_attention(pt, q_rows, bias, caches, new_rows, p, l, lam_init, pg)
        take = lambda o: o[:, :n_new].reshape(ns_tok, o.shape[2])
        xs = _post(xs, take(soa), take(sob), take(soc), p, tm_s)
        xs = _ffn(xs, p["norm_ffn2"], *p["ffn2"], tm_s)
        rows_s.append(sr)

    def stack(rows, name, lead, tail):
        z = jnp.stack([r[name] for r in rows])
        if name == "misc":
            z = z[:, :, :N_HEADS]
        return z.reshape((depth,) + lead + tail)

    lead_p, lead_s = (bsz, seq), (nb, n_new)
    hd = (N_HEADS, HEAD_DIM)
    out = [xp.reshape(bsz, seq, d), xs.reshape(nb, n_new, d)]
    for name, tail in (("akf", hd), ("avf", hd), ("misc", (N_HEADS,)), ("bkf", hd), ("bvf", hd),
                       ("bif", (HEAD_DIM,)), ("ckf", (N_HEADS, 2, HEAD_DIM)), ("cvf", (N_HEADS, 2 * HEAD_DIM))):
        out.append(stack(rows_p, name, lead_p, tail))
        out.append(stack(rows_s, name, lead_s, tail))
    return tuple(out)
```

```python
import functools
import math

import jax
import jax.numpy as jnp
from jax import lax
from jax.experimental import pallas as pl
from jax.experimental.pallas import tpu as pltpu

F32 = jnp.float32
BF16 = jnp.bfloat16

HEAD_DIM = 64
ROPE_DIM = HEAD_DIM // 4
ROPE_THETA = 500000.0
NORM_EPS = 1e-6
TOPK_MAX = 256
PAGE_SIZE = 128
N_HEADS = 4
HW = N_HEADS * HEAD_DIM
CW = 2 * HW
MISC_W = 128
NEG = -1e30
SUB = 8
VMEM_LIMIT = 56 * 1024 * 1024


def _cparams(*sem):
    return pltpu.CompilerParams(dimension_semantics=sem, vmem_limit_bytes=VMEM_LIMIT)


def _const_spec(shape):
    nd = len(shape)
    return pl.BlockSpec(shape, lambda *_: (0,) * nd, pipeline_mode=pl.Buffered(1))


def _rms(x, g):
    ms = jnp.mean(x * x, axis=-1, keepdims=True)
    return x * lax.rsqrt(ms + NORM_EPS) * g


def _sigmoid(x):
    return 1.0 / (1.0 + jnp.exp(-x))


def _lane_head(shape, width=HEAD_DIM):
    return lax.broadcasted_iota(jnp.int32, shape, len(shape) - 1) // width


def _ffn_body(x_ref, g_ref, wg_ref, wu_ref, wd_ref, o_ref, acc_ref, *, fc):
    x = x_ref[...]
    h = _rms(x, g_ref[...]).astype(BF16)
    dff = wg_ref.shape[1]
    for c in range(dff // fc):
        sl = slice(c * fc, (c + 1) * fc)
        g = jnp.dot(h, wg_ref[:, sl], preferred_element_type=F32)
        u = jnp.dot(h, wu_ref[:, sl], preferred_element_type=F32)
        a = (g * _sigmoid(g) * u).astype(BF16)
        d = jnp.dot(a, wd_ref[sl, :], preferred_element_type=F32)
        if c == 0:
            acc_ref[...] = d
        else:
            acc_ref[...] += d
    o_ref[...] = x + 0.5 * acc_ref[...]


def _ffn(x, g, wg, wu, wd, tm):
    n, d = x.shape
    dff = wg.shape[1]
    return pl.pallas_call(
        functools.partial(_ffn_body, fc=256),
        grid=(n // tm,),
        in_specs=[pl.BlockSpec((tm, d), lambda i: (i, 0)),
                  _const_spec((1, d)), _const_spec((d, dff)), _const_spec((d, dff)), _const_spec((dff, d))],
        out_specs=pl.BlockSpec((tm, d), lambda i: (i, 0)),
        out_shape=jax.ShapeDtypeStruct((n, d), F32),
        scratch_shapes=[pltpu.VMEM((tm, d), F32)],
        compiler_params=_cparams("arbitrary"),
        name="ffn",
    )(x, g, wg, wu, wd)


_O_AQ, _O_AK, _O_AV = 0, HW, 2 * HW
_O_BQ, _O_BK, _O_BV, _O_BIQ, _O_BIK = 3 * HW, 4 * HW, 5 * HW, 6 * HW, 7 * HW
_O_CQ, _O_CK, _O_CV = 8 * HW, 8 * HW + CW, 8 * HW + 2 * CW
_O_MISC = 8 * HW + 3 * CW
PACK_W = _O_MISC + MISC_W


def _pack_w_in(w):
    widths = (HW, HW, HW, N_HEADS, HW, HW, HW, HW, HEAD_DIM, N_HEADS, CW, CW, CW)
    offs = [0]
    for wd in widths:
        offs.append(offs[-1] + wd)
    part = lambda i: w[:, offs[i]:offs[i + 1]]
    aq, ak, av, af, bq, bk, bv, biq, bik, biw, cq, ck, cv = [part(i) for i in range(13)]
    misc = jnp.concatenate([af, biw, jnp.zeros((w.shape[0], MISC_W - 2 * N_HEADS), w.dtype)], axis=1)
    return jnp.concatenate([aq, ak, av, bq, bk, bv, biq, jnp.tile(bik, (1, N_HEADS)), cq, ck, cv, misc], axis=1)


def _headnorm(x, bd_ref, g):
    sq = x * x
    hi = sq.astype(BF16)
    lo = (sq - hi.astype(F32)).astype(BF16)
    ms = (jnp.dot(hi, bd_ref[...], preferred_element_type=F32)
          + jnp.dot(lo, bd_ref[...], preferred_element_type=F32))
    return x * lax.rsqrt(ms + NORM_EPS) * g


def _rope(x, cos, sa, sb):
    w = x.shape[1]
    rep = w // cos.shape[1]
    if rep > 1:
        cos, sa, sb = (jnp.concatenate([t] * rep, axis=1) for t in (cos, sa, sb))
    half = ROPE_DIM // 2
    return x * cos + pltpu.roll(x, half, axis=1) * sa + pltpu.roll(x, w - half, axis=1) * sb


def _inproj_body(x_ref, g_ref, w_ref, fb_ref, gaq_ref, gak_ref, gbq_ref, gbk_ref, gcq_ref, gck_ref,
                 cos_ref, sa_ref, sb_ref, bd_ref,
                 aq_o, akf_o, akb_o, avf_o, avb_o, bq_o, bkf_o, bkb_o, bvf_o, bvb_o, biq_o, bif_o, bib_o,
                 cq_o, ckf_o, ckb_o, cvf_o, cvb_o, misc_o, cum_o, carry_ref, *, tiles_per_seq, scale):
    i = pl.program_id(0)
    x = x_ref[...]
    h = _rms(x, g_ref[...]).astype(BF16)
    proj = lambda off, wd: jnp.dot(h, w_ref[:, off:off + wd], preferred_element_type=F32)
    cos, sa, sb = cos_ref[...], sa_ref[...], sb_ref[...]

    aq = _headnorm(proj(_O_AQ, HW), bd_ref, gaq_ref[...])
    aq_o[...] = (aq * scale).astype(BF16)
    ak = _headnorm(proj(_O_AK, HW), bd_ref, gak_ref[...])
    akf_o[...] = ak
    akb_o[...] = ak.astype(BF16)
    av = proj(_O_AV, HW)
    avf_o[...] = av
    avb_o[...] = av.astype(BF16)

    bq = _rope(_headnorm(proj(_O_BQ, HW), bd_ref, gbq_ref[...]), cos, sa, sb)
    bq_o[...] = (bq * scale).astype(BF16)
    bk = _rope(_headnorm(proj(_O_BK, HW), bd_ref, gbk_ref[...]), cos, sa, sb)
    bkf_o[...] = bk
    bkb_o[...] = bk.astype(BF16)
    bv = proj(_O_BV, HW)
    bvf_o[...] = bv
    bvb_o[...] = bv.astype(BF16)
    biq_o[...] = _rope(proj(_O_BIQ, HW), cos, sa, sb).astype(BF16)
    bik = _rope(proj(_O_BIK, HW), cos, sa, sb)
    bif_o[...] = bik[:, :HEAD_DIM]
    bib_o[...] = bik.astype(BF16)

    for half in range(2):
        sl = slice(half * HW, (half + 1) * HW)
        cq = _rope(_headnorm(proj(_O_CQ + half * HW, HW), bd_ref, gcq_ref[:, sl]), cos, sa, sb)
        cq_o[:, sl] = (cq * scale).astype(BF16)
        ck = _rope(_headnorm(proj(_O_CK + half * HW, HW), bd_ref, gck_ref[:, sl]), cos, sa, sb)
        ckf_o[:, sl] = ck
        ckb_o[:, sl] = ck.astype(BF16)
    cv = proj(_O_CV, CW)
    cvf_o[...] = cv
    cvb_o[...] = cv.astype(BF16)

    misc = proj(_O_MISC, MISC_W)
    z = misc + fb_ref[...]
    logf = jnp.minimum(z, 0.0) - jnp.log(1.0 + jnp.exp(-jnp.abs(z)))
    lane = lax.broadcasted_iota(jnp.int32, misc.shape, 1)
    misc = jnp.where(lane < N_HEADS, logf, misc)
    misc_o[...] = misc

    tm = misc.shape[0]
    r = lax.broadcasted_iota(jnp.int32, (tm, tm), 0)
    c = lax.broadcasted_iota(jnp.int32, (tm, tm), 1)
    tri = jnp.where(c <= r, 1.0, 0.0).astype(F32)

    @pl.when(i % tiles_per_seq == 0)
    def _():
        carry_ref[...] = jnp.zeros_like(carry_ref)

    cum = jnp.dot(tri, misc, preferred_element_type=F32, precision=lax.Precision.HIGHEST) + carry_ref[...]
    cum_o[...] = cum
    carry_ref[...] = cum[tm - 1:tm, :]


def _inproj(x, p, tabs, tm, seq_len):
    n, d = x.shape
    cos, sa, sb = tabs
    tab_tiles = cos.shape[0] // tm
    row = lambda w: pl.BlockSpec((tm, w), lambda i: (i, 0))
    tab = pl.BlockSpec((tm, cos.shape[1]), lambda i: (i % tab_tiles, 0))
    sds = lambda w, dt: jax.ShapeDtypeStruct((n, w), dt)
    outs = [
        (HW, BF16),
        (HW, F32), (HW, BF16), (HW, F32), (HW, BF16),
        (HW, BF16),
        (HW, F32), (HW, BF16), (HW, F32), (HW, BF16),
        (HW, BF16),
        (HEAD_DIM, F32), (HW, BF16),
        (CW, BF16),
        (CW, F32), (CW, BF16), (CW, F32), (CW, BF16),
        (MISC_W, F32), (MISC_W, F32),
    ]
    return pl.pallas_call(
        functools.partial(_inproj_body, tiles_per_seq=max(seq_len // tm, 1), scale=HEAD_DIM ** -0.5),
        grid=(n // tm,),
        in_specs=[row(d), _const_spec((1, d)), _const_spec((d, PACK_W)), _const_spec((1, MISC_W)),
                  _const_spec((1, HW)), _const_spec((1, HW)), _const_spec((1, HW)), _const_spec((1, HW)),
                  _const_spec((1, CW)), _const_spec((1, CW)), tab, tab, tab, _const_spec((HW, HW))],
        out_specs=[row(w) for w, _ in outs],
        out_shape=[sds(w, dt) for w, dt in outs],
        scratch_shapes=[pltpu.VMEM((1, MISC_W), F32)],
        compiler_params=_cparams("arbitrary"),
        name="inproj",
    )(x, p["norm_mix"], p["w_in"], p["fox_bias"], p["gaq"], p["gak"], p["gbq"], p["gbk"], p["gcq"], p["gck"],
      cos, sa, sb, p["bd"])


def _online_update(s, m_prev, l_prev):
    m_new = jnp.maximum(m_prev, jnp.max(s, axis=-1, keepdims=True))
    alpha = jnp.exp(m_prev - m_new)
    p = jnp.exp(s - m_new)
    l_new = alpha * l_prev + jnp.sum(p, axis=-1, keepdims=True)
    return p, m_new, l_new, alpha


def _qk(q, k):
    return lax.dot_general(q, k, (((1,), (1,)), ((), ())), preferred_element_type=F32)


def _head_masks(dtype):
    return [(_lane_head((1, HW)) == h).astype(dtype) for h in range(N_HEADS)]


def _online_update_t(s, m_prev, l_prev):
    m_new = jnp.maximum(m_prev, jnp.max(s, axis=0, keepdims=True))
    alpha = jnp.exp(m_prev - m_new)
    p = jnp.exp(s - m_new)
    l_new = alpha * l_prev + jnp.sum(p, axis=0, keepdims=True)
    return p, m_new, l_new, alpha


def _softmax_state_init(m_ref, l_ref, acc_ref):
    m_ref[...] = jnp.full(m_ref.shape, NEG, F32)
    l_ref[...] = jnp.zeros(l_ref.shape, F32)
    acc_ref[...] = jnp.zeros(acc_ref.shape, F32)


def _causal_tiles(i, tq, tk):
    qpos = i * tq + lax.broadcasted_iota(jnp.int32, (tk, tq), 1)
    kiota = lax.broadcasted_iota(jnp.int32, (tk, tq), 0)
    return qpos, kiota, (i * tq + tq + tk - 1) // tk


def _by_head(cols):
    lh = _lane_head((cols[0].shape[0], HW))
    out = cols[N_HEADS - 1]
    for h in range(N_HEADS - 2, -1, -1):
        out = jnp.where(lh == h, cols[h], out)
    return out


def _heads_init(q, qs_ref, m_ref, l_ref, acc_ref):
    for h, hm in enumerate(_head_masks(q.dtype)):
        qs_ref[h] = q * hm
    _softmax_state_init(m_ref, l_ref, acc_ref)


def _heads_chunk(qs_ref, k, v, logit_fn, m_ref, l_ref, acc_ref):
    ps, alphas = [], []
    for h in range(N_HEADS):
        s = logit_fn(h, _qk(qs_ref[h], k))
        p, m_new, l_new, alpha = _online_update(s, m_ref[h], l_ref[h])
        m_ref[h] = m_new
        l_ref[h] = l_new
        ps.append(p.astype(BF16))
        alphas.append(alpha)
    p_cat = jnp.concatenate(ps, axis=1)
    v_bd = jnp.concatenate([v * hm for hm in _head_masks(v.dtype)], axis=0)
    acc_ref[...] = _by_head(alphas) * acc_ref[...] + jnp.dot(p_cat, v_bd, preferred_element_type=F32)


def _heads_finish(l_ref, acc_ref):
    return acc_ref[...] / _by_head([l_ref[h] for h in range(N_HEADS)])


def _fox_body(q_ref, k_ref, v_ref, kc_ref, o_ref, qs_ref, m_ref, l_ref, acc_ref, *, tq, tk):
    i = pl.program_id(1)
    _heads_init(q_ref[0], qs_ref, m_ref, l_ref, acc_ref)
    qpos = i * tq + lax.broadcasted_iota(jnp.int32, (tq, tk), 0)
    kiota = lax.broadcasted_iota(jnp.int32, (tq, tk), 1)
    nk = (i * tq + tq + tk - 1) // tk

    def chunk(j, carry):
        off = pl.multiple_of(j * tk, tk)
        visible = (kiota + off) <= qpos
        logit = lambda h, s: jnp.where(visible, s - kc_ref[0, h:h + 1, pl.ds(off, tk)], NEG)
        _heads_chunk(qs_ref, k_ref[0, pl.ds(off, tk), :], v_ref[0, pl.ds(off, tk), :], logit,
                     m_ref, l_ref, acc_ref)
        return carry

    lax.fori_loop(0, nk, chunk, 0)
    o_ref[0] = _heads_finish(l_ref, acc_ref).astype(BF16)


def _heads_specs(b, t, w, tq):
    tile = pl.BlockSpec((1, tq, w), lambda bi, i: (bi, i, 0))
    keys = pl.BlockSpec((1, t, w), lambda bi, i: (bi, 0, 0))
    scratch = [pltpu.VMEM((N_HEADS, tq, w), BF16), pltpu.VMEM((N_HEADS, tq, 1), F32),
               pltpu.VMEM((N_HEADS, tq, 1), F32), pltpu.VMEM((tq, w), F32)]
    return tile, keys, scratch


def _fox_prompt(q, k, v, kcum, tq, tk):
    b, t, w = q.shape
    tile, keys, scratch = _heads_specs(b, t, w, tq)
    return pl.pallas_call(
        functools.partial(_fox_body, tq=tq, tk=tk),
        grid=(b, t // tq),
        in_specs=[tile, keys, keys, pl.BlockSpec((1, SUB, t), lambda bi, i: (bi, 0, 0))],
        out_specs=tile,
        out_shape=jax.ShapeDtypeStruct((b, t, w), BF16),
        scratch_shapes=scratch,
        compiler_params=_cparams("arbitrary", "arbitrary"),
        name="fox_prompt",
    )(q, k, v, kcum)


def _diff_finish(acc1, l1, acc2, l2, lam, gain, out_scale):
    o = acc1 / l1 - lam * (acc2 / l2)
    return _rms(o, gain) * out_scale


def _lambda(lq1, lk1, lq2, lk2, lam_init):
    return (jnp.exp(jnp.sum(lq1 * lk1, axis=-1, keepdims=True))
            - jnp.exp(jnp.sum(lq2 * lk2, axis=-1, keepdims=True)) + lam_init)


def _diff_body(q_ref, k_ref, vt_ref, lq1_ref, lk1_ref, lq2_ref, lk2_ref, sgt_ref, o_ref,
               qs_ref, m_ref, l_ref, acc_ref, *, tq, tk, lam_init):
    nmap = 2 * N_HEADS
    for half in range(2):
        q = q_ref[0, :, half * HW:(half + 1) * HW]
        for g, hm in enumerate(_head_masks(q.dtype)):
            qs_ref[half * N_HEADS + g] = q * hm
    _softmax_state_init(m_ref, l_ref, acc_ref)
    qpos, kiota, nk = _causal_tiles(pl.program_id(1), tq, tk)
    vw = 2 * HEAD_DIM

    def chunk(j, carry):
        off = pl.multiple_of(j * tk, tk)
        visible = (kiota + off) <= qpos
        for mp in range(nmap):
            half, head = mp // N_HEADS, mp // 2
            k = k_ref[0, pl.ds(off, tk), half * HW:(half + 1) * HW]
            vt = vt_ref[0, head * vw:(head + 1) * vw, pl.ds(off, tk)]
            s = jnp.where(visible, _qk(k, qs_ref[mp]), NEG)
            p, m_new, l_new, alpha = _online_update_t(s, m_ref[mp], l_ref[mp])
            m_ref[mp] = m_new
            l_ref[mp] = l_new
            acc_ref[mp] = alpha * acc_ref[mp] + jnp.dot(vt, p.astype(BF16), preferred_element_type=F32)
        return carry

    lax.fori_loop(0, nk, chunk, 0)
    lam = _lambda(lq1_ref[...], lk1_ref[...], lq2_ref[...], lk2_ref[...], lam_init)
    outs = []
    for head in range(N_HEADS):
        o = acc_ref[2 * head] / l_ref[2 * head] - lam * (acc_ref[2 * head + 1] / l_ref[2 * head + 1])
        ms = jnp.mean(o * o, axis=0, keepdims=True)
        outs.append(o * lax.rsqrt(ms + NORM_EPS) * sgt_ref[...] * (1.0 - lam_init))
    o_ref[0] = jnp.concatenate(outs, axis=0).T.astype(BF16)


def _diff_prompt(q, k, vt, p, lam_init, tq, tk):
    b, t, w = q.shape
    nmap = 2 * N_HEADS
    vec = _const_spec((1, HEAD_DIM))
    tile = pl.BlockSpec((1, tq, w), lambda bi, i: (bi, i, 0))
    return pl.pallas_call(
        functools.partial(_diff_body, tq=tq, tk=tk, lam_init=lam_init),
        grid=(b, t // tq),
        in_specs=[tile,
                  pl.BlockSpec((1, t, w), lambda bi, i: (bi, 0, 0)),
                  pl.BlockSpec((1, w, t), lambda bi, i: (bi, 0, 0)),
                  vec, vec, vec, vec, _const_spec((2 * HEAD_DIM, 1))],
        out_specs=tile,
        out_shape=jax.ShapeDtypeStruct((b, t, w), BF16),
        scratch_shapes=[pltpu.VMEM((nmap, tq, HW), BF16), pltpu.VMEM((nmap, 1, tq), F32),
                        pltpu.VMEM((nmap, 1, tq), F32), pltpu.VMEM((nmap, 2 * HEAD_DIM, tq), F32)],
        compiler_params=_cparams("arbitrary", "arbitrary"),
        name="diff_prompt",
    )(q, k, vt, p["lq1"], p["lk1"], p["lq2"], p["lk2"], p["c_subln"].reshape(-1, 1))


INT_MIN = -2 ** 31
KEY_NEG_INF = int(0xFF800000 - 2 ** 32) ^ 0x7FFFFFFF


def _sort_key(score):
    bits = lax.bitcast_convert_type(score, jnp.int32)
    return jnp.where(bits < 0, bits ^ jnp.int32(0x7FFFFFFF), bits)


COUNT_UNROLL = 4
COUNT_W = COUNT_UNROLL * 128


def _count_keys(keys_ref, n_iter, hit, operands):
    nq = keys_ref.shape[1]
    pos = lax.broadcasted_iota(jnp.int32, (128, nq), 0)

    def chunk(c, acc):
        for u in range(COUNT_UNROLL):
            off = pl.multiple_of(c * COUNT_W + u * 128, 128)
            h = hit(keys_ref[pl.ds(off, 128), :], pos + off, *operands)
            acc = acc + sum(h[r:r + SUB] for r in range(0, 128, SUB))
        return acc

    acc = lax.fori_loop(0, n_iter, chunk, jnp.zeros((SUB, nq), F32))
    return jnp.sum(acc, axis=0, keepdims=True)


def _topk_select(keys_ref, n_iter, topk, pos_bits):
    nq = keys_ref.shape[1]
    kf = float(topk)
    one = lambda m: jnp.where(m, 1.0, 0.0)
    count = functools.partial(_count_keys, keys_ref, n_iter)
    cnt0 = count(lambda kk, pos: one(kk >= 0), [])
    thr0 = jnp.where(cnt0 >= kf, jnp.int32(0), jnp.int32(INT_MIN))

    def bit_step(t, thr):
        cand = thr | jnp.left_shift(jnp.int32(1), 30 - t)
        cnt = count(lambda kk, pos, cd: one(kk >= cd), [cand])
        return jnp.where(cnt >= kf, cand, thr)

    thr = lax.fori_loop(0, 31, bit_step, thr0)
    n_gt = count(lambda kk, pos, th: one(kk > th), [thr])
    n_ge = count(lambda kk, pos, th: one(kk >= th), [thr])
    need = kf - n_gt
    excess = jnp.where(n_ge > kf, one(thr > KEY_NEG_INF), 0.0)
    all_pos = jnp.full((1, nq), 2 ** pos_bits - 1, jnp.int32)

    def tie_search():
        def pos_step(t, x):
            cand = x | jnp.left_shift(jnp.int32(1), pos_bits - 1 - t)
            cnt = count(lambda kk, pos, th, cd: jnp.where(kk == th, one(pos < cd), 0.0), [thr, cand])
            return jnp.where(cnt < need, cand, x)
        return lax.fori_loop(0, pos_bits, pos_step, jnp.zeros((1, nq), jnp.int32))

    xpos = lax.cond(jnp.max(excess) > 0.0, tie_search, lambda: all_pos)
    return thr, xpos


def _select_bias(kk, pos, thr, xpos):
    tie = jnp.where(pos <= xpos, 0.0, NEG)
    sel = jnp.where(kk > thr, 0.0, jnp.where(kk == thr, tie, NEG))
    return jnp.where(kk > KEY_NEG_INF, sel, NEG)


def _dsa_body(q_ref, iq_ref, iw_ref, k_ref, v_ref, ik_ref, o_ref,
              qs_ref, iqs_ref, keys_ref, m_ref, l_ref, acc_ref, *, tq, tk, topk, pos_bits):
    iq = iq_ref[0]
    for h, hm in enumerate(_head_masks(iq.dtype)):
        iqs_ref[h] = iq * hm
    qpos, kiota, nk = _causal_tiles(pl.program_id(1), tq, tk)
    iw = iw_ref[0]

    def score_chunk(j, carry):
        off = pl.multiple_of(j * tk, tk)
        ik = ik_ref[0, pl.ds(off, tk), :]
        score = jnp.zeros((tk, tq), F32)
        for h in range(N_HEADS):
            score = score + iw[h:h + 1, :] * jnp.maximum(_qk(ik, iqs_ref[h]), 0.0)
        score = jnp.where((kiota + off) <= qpos, score, -jnp.inf)
        keys_ref[pl.ds(off, tk), :] = _sort_key(score)
        return carry

    lax.fori_loop(0, nk, score_chunk, 0)
    thr, xpos = _topk_select(keys_ref, nk * (tk // COUNT_W), topk, pos_bits)

    _heads_init(q_ref[0], qs_ref, m_ref, l_ref, acc_ref)

    def chunk(j, carry):
        off = pl.multiple_of(j * tk, tk)
        bias = _select_bias(keys_ref[pl.ds(off, tk), :], kiota + off, thr, xpos).T
        _heads_chunk(qs_ref, k_ref[0, pl.ds(off, tk), :], v_ref[0, pl.ds(off, tk), :],
                     lambda h, s: s + bias, m_ref, l_ref, acc_ref)
        return carry

    lax.fori_loop(0, nk, chunk, 0)
    o_ref[0] = _heads_finish(l_ref, acc_ref).astype(BF16)


def _dsa_prompt(q, iq, iw_t, k, v, ik, tq, tk):
    b, t, w = q.shape
    topk = min(TOPK_MAX, t // 4)
    assert tk >= topk and tk % COUNT_W == 0
    tile, keys, scratch = _heads_specs(b, t, w, tq)
    return pl.pallas_call(
        functools.partial(_dsa_body, tq=tq, tk=tk, topk=topk, pos_bits=max(t - 1, 1).bit_length()),
        grid=(b, t // tq),
        in_specs=[tile, tile, pl.BlockSpec((1, SUB, tq), lambda bi, i: (bi, 0, i)), keys, keys, keys],
        out_specs=tile,
        out_shape=jax.ShapeDtypeStruct((b, t, w), BF16),
        scratch_shapes=[scratch[0], pltpu.VMEM((N_HEADS, tq, w), BF16), pltpu.VMEM((t, tq), jnp.int32)] + scratch[1:],
        compiler_params=_cparams("arbitrary", "arbitrary"),
        name="dsa_prompt",
    )(q, iq, iw_t, k, v, ik)


def _post_body(x_ref, oa_ref, ob_ref, oc_ref, g_ref, wgate_ref, bgate_ref, woa_ref, wob_ref, woc_ref, wout_ref,
               o_ref, acc_ref, *, mc):
    x = x_ref[...]
    d = x.shape[1]
    h = _rms(x, g_ref[...]).astype(BF16)
    oa, ob, oc = oa_ref[...], ob_ref[...], oc_ref[...]
    for c in range(d // mc):
        merged = None
        for br, (o, w_ref) in enumerate(((oa, woa_ref), (ob, wob_ref), (oc, woc_ref))):
            sl = slice(br * d + c * mc, br * d + (c + 1) * mc)
            gate = _sigmoid(jnp.dot(h, wgate_ref[:, sl], preferred_element_type=F32) + bgate_ref[:, sl])
            term = gate * jnp.dot(o, w_ref[:, c * mc:(c + 1) * mc], preferred_element_type=F32)
            merged = term if merged is None else merged + term
        part = jnp.dot(merged.astype(BF16), wout_ref[c * mc:(c + 1) * mc, :], preferred_element_type=F32)
        if c == 0:
            acc_ref[...] = part
        else:
            acc_ref[...] += part
    o_ref[...] = x + acc_ref[...]


def _post(x, oa, ob, oc, p, tm):
    n, d = x.shape
    row = lambda w: pl.BlockSpec((tm, w), lambda i: (i, 0))
    return pl.pallas_call(
        functools.partial(_post_body, mc=256),
        grid=(n // tm,),
        in_specs=[row(d), row(HW), row(HW), row(CW), _const_spec((1, d)),
                  _const_spec((d, 3 * d)), _const_spec((1, 3 * d)),
                  _const_spec((HW, d)), _const_spec((HW, d)), _const_spec((CW, d)), _const_spec((d, d))],
        out_specs=row(d),
        out_shape=jax.ShapeDtypeStruct((n, d), F32),
        scratch_shapes=[pltpu.VMEM((tm, d), F32)],
        compiler_params=_cparams("arbitrary"),
        name="post",
    )(x, oa, ob, oc, p["norm_mix"], p["w_gate"], p["b_gate"], p["w_o_a"], p["w_o_b"], p["w_o_c"], p["w_out"])


def _expand_rows(x, reps):
    return jnp.concatenate([jnp.broadcast_to(x[g:g + 1], (SUB, x.shape[1])) for g in range(reps)], axis=0)


def _new_visible():
    r = lax.broadcasted_iota(jnp.int32, (SUB, PAGE_SIZE), 0)
    c = lax.broadcasted_iota(jnp.int32, (SUB, PAGE_SIZE), 1)
    return c <= r


def _sscore_body(pt_ref, iq_ref, iw_ref, *refs, npg):
    pages, new_ref, keys_ref = refs[:npg], refs[npg], refs[npg + 1]

    def scores(kt):
        sc = jnp.maximum(jnp.dot(iq_ref[0], kt, preferred_element_type=F32), 0.0) * iw_ref[0]
        return sc[0:SUB] + sc[SUB:2 * SUB] + sc[2 * SUB:3 * SUB] + sc[3 * SUB:4 * SUB]

    for j in range(npg):
        keys_ref[0, :, j * PAGE_SIZE:(j + 1) * PAGE_SIZE] = _sort_key(scores(pages[j][...].astype(BF16)))
    s_new = jnp.where(_new_visible(), scores(new_ref[0]), -jnp.inf)
    keys_ref[0, :, npg * PAGE_SIZE:(npg + 1) * PAGE_SIZE] = _sort_key(s_new)
    pad = keys_ref.shape[2] - (npg + 1) * PAGE_SIZE
    if pad:
        keys_ref[0, :, (npg + 1) * PAGE_SIZE:] = jnp.full((SUB, pad), KEY_NEG_INF, jnp.int32)


def _sample_scores(pt, iq_rows, iw_rows, idx_t, new_idx_t, layer, ncols):
    nb = iq_rows.shape[0]
    npg = pt.shape[0] // nb
    page = lambda j: pl.BlockSpec((None, None, HEAD_DIM, PAGE_SIZE),
                                  lambda b, pt: (layer, pt[b * npg + j], 0, 0))
    return pl.pallas_call(
        functools.partial(_sscore_body, npg=npg),
        grid_spec=pltpu.PrefetchScalarGridSpec(
            num_scalar_prefetch=1,
            grid=(nb,),
            in_specs=[pl.BlockSpec((1, N_HEADS * SUB, HEAD_DIM), lambda b, pt: (b, 0, 0)),
                      pl.BlockSpec((1, N_HEADS * SUB, 1), lambda b, pt: (b, 0, 0))]
                     + [page(j) for j in range(npg)]
                     + [pl.BlockSpec((1, HEAD_DIM, PAGE_SIZE), lambda b, pt: (b, 0, 0))],
            out_specs=pl.BlockSpec((1, SUB, ncols), lambda b, pt: (b, 0, 0))),
        out_shape=jax.ShapeDtypeStruct((nb, SUB, ncols), jnp.int32),
        compiler_params=_cparams("arbitrary"),
        name="sample_scores",
    )(pt, iq_rows, iw_rows, *([idx_t] * npg), new_idx_t)


def _sselect_body(keys_ref, bias_ref, *, topk, pos_bits):
    nkeys, nq = keys_ref.shape
    thr, xpos = _topk_select(keys_ref, nkeys // COUNT_W, topk, pos_bits)
    pos = lax.broadcasted_iota(jnp.int32, (128, nq), 0)
    for g in range(nkeys // 128):
        sl = slice(g * 128, (g + 1) * 128)
        bias_ref[sl, :] = _select_bias(keys_ref[sl, :], pos + g * 128, thr, xpos)


def _sample_select(keys_t, topk):
    nkeys, nq = keys_t.shape
    tl = min(256, nq)
    return pl.pallas_call(
        functools.partial(_sselect_body, topk=topk, pos_bits=max(nkeys - 1, 1).bit_length()),
        grid=(nq // tl,),
        in_specs=[pl.BlockSpec((nkeys, tl), lambda i: (0, i))],
        out_specs=pl.BlockSpec((nkeys, tl), lambda i: (0, i)),
        out_shape=jax.ShapeDtypeStruct((nkeys, nq), F32),
        compiler_params=_cparams("arbitrary"),
        name="sample_select",
    )(keys_t)


def _sattn_body(pt_ref, aq_ref, bq_ref, cq_ref, bias_ref, biasn_ref, *refs, pg, n_steps, lam_init):
    it = iter(refs)
    take = lambda n: [next(it) for _ in range(n)]
    ak_refs, av_refs, lf_refs, bk_refs, bv_refs, ck_refs, cv_refs = (take(pg) for _ in range(7))
    akn_ref, avn_ref, lfn_ref, bkn_ref, bvn_ref, ckn_ref, cvn_ref = take(7)
    lq1_ref, lk1_ref, lq2_ref, lk2_ref, sg_ref = take(5)
    oa_ref, ob_ref, oc_ref = take(3)
    ma_ref, la_ref, acca_ref, mb_ref, lb_ref, accb_ref, mc_ref, lc_ref, accc_ref, ctot_ref = take(10)
    p = pl.program_id(1)
    nmap = 2 * N_HEADS

    @pl.when(p == 0)
    def _():
        for m_ref, l_ref, acc_ref in ((ma_ref, la_ref, acca_ref), (mb_ref, lb_ref, accb_ref),
                                      (mc_ref, lc_ref, accc_ref)):
            m_ref[...] = jnp.full(m_ref.shape, NEG, F32)
            l_ref[...] = jnp.zeros(l_ref.shape, F32)
            acc_ref[...] = jnp.zeros(acc_ref.shape, F32)
        ctot_ref[...] = jnp.zeros(ctot_ref.shape, F32)

    r = lax.broadcasted_iota(jnp.int32, (PAGE_SIZE, PAGE_SIZE), 0)
    c = lax.broadcasted_iota(jnp.int32, (PAGE_SIZE, PAGE_SIZE), 1)
    tri = jnp.where(r <= c, 1.0, 0.0).astype(F32)
    lanes = lambda xs: jnp.concatenate(xs, axis=1) if len(xs) > 1 else xs[0]
    page_cols = lambda x, j: x[:, j * PAGE_SIZE:(j + 1) * PAGE_SIZE]

    def step(akt, avt, lfs, bkt, bvt, ckt, cvh, bias, vis):
        n = len(akt)
        pad = jnp.zeros((SUB - N_HEADS, PAGE_SIZE), F32)
        in_page = [jnp.dot(jnp.concatenate([lf, pad], axis=0), tri, preferred_element_type=F32,
                           precision=lax.Precision.HIGHEST) for lf in lfs]
        tot = ctot_ref[...]
        cums = []
        for j in range(n):
            cum = in_page[j] + tot
            tot = cum[:, PAGE_SIZE - 1:PAGE_SIZE]
            cums.append(_expand_rows(cum, N_HEADS))
        ctot_ref[...] = tot
        dots = lambda q, kts: lanes([jnp.dot(q, kt, preferred_element_type=F32) for kt in kts])
        s = dots(aq_ref[0], akt) - lanes(cums)
        if vis is not None:
            s = jnp.where(jnp.concatenate([vis] * N_HEADS, axis=0), s, NEG)
        pa, m_new, l_new, alpha = _online_update(s, ma_ref[...], la_ref[...])
        ma_ref[...] = m_new
        la_ref[...] = l_new
        pa = pa.astype(BF16)
        acca_ref[...] = alpha * acca_ref[...] + sum(_qk(page_cols(pa, j), avt[j]) for j in range(n))
        s = dots(bq_ref[0], bkt) + jnp.concatenate([bias] * N_HEADS, axis=0)
        pb, m_new, l_new, alpha = _online_update(s, mb_ref[...], lb_ref[...])
        mb_ref[...] = m_new
        lb_ref[...] = l_new
        pb = pb.astype(BF16)
        accb_ref[...] = alpha * accb_ref[...] + sum(_qk(page_cols(pb, j), bvt[j]) for j in range(n))
        s = dots(cq_ref[0], ckt)
        if vis is not None:
            s = jnp.where(jnp.concatenate([vis] * nmap, axis=0), s, NEG)
        pc, m_new, l_new, alpha = _online_update(s, mc_ref[...], lc_ref[...])
        mc_ref[...] = m_new
        lc_ref[...] = l_new
        pc = pc.astype(BF16)
        for head in range(N_HEADS):
            rows = slice(head * 2 * SUB, (head + 1) * 2 * SUB)
            pv = sum(jnp.dot(page_cols(pc[rows], j), cvh[j][head], preferred_element_type=F32) for j in range(n))
            accc_ref[rows, :] = alpha[rows] * accc_ref[rows, :] + pv

    bf = lambda refs_: [rf[...].astype(BF16) for rf in refs_]
    step(bf(ak_refs), bf(av_refs), [rf[...] for rf in lf_refs], bf(bk_refs), bf(bv_refs), bf(ck_refs),
         [[rf[pl.ds(head, PAGE_SIZE, stride=N_HEADS), :].astype(BF16) for head in range(N_HEADS)]
          for rf in cv_refs], bias_ref[0], None)

    @pl.when(p == n_steps - 1)
    def _():
        cvn = cvn_ref[0]
        step([akn_ref[0]], [avn_ref[0]], [lfn_ref[0]], [bkn_ref[0]], [bvn_ref[0]], [ckn_ref[0]],
             [[cvn[:, head * 2 * HEAD_DIM:(head + 1) * 2 * HEAD_DIM] for head in range(N_HEADS)]],
             biasn_ref[0], _new_visible())
        lho = _lane_head((SUB, HW))
        for acc_ref, l_ref, o_ref in ((acca_ref, la_ref, oa_ref), (accb_ref, lb_ref, ob_ref)):
            o = acc_ref[...] / l_ref[...]
            out = jnp.zeros((SUB, HW), F32)
            for h in range(N_HEADS):
                out = jnp.where(lho == h, o[h * SUB:(h + 1) * SUB], out)
            o_ref[0] = out.astype(BF16)
        lam = _lambda(lq1_ref[...], lk1_ref[...], lq2_ref[...], lk2_ref[...], lam_init)
        accc = accc_ref[...]
        lc = lc_ref[...]
        for head in range(N_HEADS):
            r1 = slice(head * 2 * SUB, head * 2 * SUB + SUB)
            r2 = slice(head * 2 * SUB + SUB, (head + 1) * 2 * SUB)
            o = _diff_finish(accc[r1], lc[r1], accc[r2], lc[r2], lam, sg_ref[...], 1.0 - lam_init)
            oc_ref[0, :, head * 2 * HEAD_DIM:(head + 1) * 2 * HEAD_DIM] = o.astype(BF16)


def _sample_attention(pt, q_rows, bias, caches, new_rows, p, layer, lam_init, pg):
    aq, bq, cq = q_rows
    nb = aq.shape[0]
    npg = pt.shape[0] // nb
    n_steps = npg // pg
    nmap = 2 * N_HEADS

    def pages(shape):
        nd = len(shape)
        return [pl.BlockSpec((None, None) + shape,
                             lambda b, p, pt, j=j: (layer, pt[b * npg + p * pg + j]) + (0,) * nd)
                for j in range(pg)]

    per_seq = lambda shape: pl.BlockSpec((1,) + shape, lambda b, p, pt: (b,) + (0,) * len(shape))
    const = lambda shape: pl.BlockSpec(shape, lambda b, p, pt: (0,) * len(shape))
    in_specs = ([per_seq((N_HEADS * SUB, HW)), per_seq((N_HEADS * SUB, HW)), per_seq((nmap * SUB, CW)),
                 pl.BlockSpec((1, SUB, pg * PAGE_SIZE), lambda b, p, pt: (b, 0, p)),
                 pl.BlockSpec((1, SUB, PAGE_SIZE), lambda b, p, pt: (b, 0, npg))]
                + pages((HW, PAGE_SIZE)) + pages((HW, PAGE_SIZE)) + pages((N_HEADS, PAGE_SIZE))
                + pages((HW, PAGE_SIZE)) + pages((HW, PAGE_SIZE)) + pages((CW, PAGE_SIZE))
                + pages((PAGE_SIZE * N_HEADS, 2 * HEAD_DIM))
                + [per_seq((HW, PAGE_SIZE)), per_seq((HW, PAGE_SIZE)), per_seq((N_HEADS, PAGE_SIZE)),
                   per_seq((HW, PAGE_SIZE)), per_seq((HW, PAGE_SIZE)), per_seq((CW, PAGE_SIZE)),
                   per_seq((PAGE_SIZE, CW)),
                   const((1, HEAD_DIM)), const((1, HEAD_DIM)), const((1, HEAD_DIM)), const((1, HEAD_DIM)),
                   const((1, 2 * HEAD_DIM))])
    cache_args = [c for c in caches for _ in range(pg)]
    out_specs = [per_seq((SUB, HW)), per_seq((SUB, HW)), per_seq((SUB, CW))]
    scratch = [pltpu.VMEM((N_HEADS * SUB, 1), F32), pltpu.VMEM((N_HEADS * SUB, 1), F32),
               pltpu.VMEM((N_HEADS * SUB, HW), F32),
               pltpu.VMEM((N_HEADS * SUB, 1), F32), pltpu.VMEM((N_HEADS * SUB, 1), F32),
               pltpu.VMEM((N_HEADS * SUB, HW), F32),
               pltpu.VMEM((nmap * SUB, 1), F32), pltpu.VMEM((nmap * SUB, 1), F32),
               pltpu.VMEM((nmap * SUB, 2 * HEAD_DIM), F32),
               pltpu.VMEM((SUB, 1), F32)]
    return pl.pallas_call(
        functools.partial(_sattn_body, pg=pg, n_steps=n_steps, lam_init=lam_init),
        grid_spec=pltpu.PrefetchScalarGridSpec(
            num_scalar_prefetch=1, grid=(nb, n_steps),
            in_specs=in_specs, out_specs=out_specs, scratch_shapes=scratch),
        out_shape=[jax.ShapeDtypeStruct((nb, SUB, HW), BF16), jax.ShapeDtypeStruct((nb, SUB, HW), BF16),
                   jax.ShapeDtypeStruct((nb, SUB, CW), BF16)],
        compiler_params=_cparams("arbitrary", "arbitrary"),
        name="sample_attention",
    )(pt, aq, bq, cq, bias, bias, *cache_args, *new_rows, p["lq1"], p["lk1"], p["lq2"], p["lk2"], p["c_subln"])


def _rope_tables(pos):
    half = ROPE_DIM // 2
    inv_freq = ROPE_THETA ** (-jnp.arange(half, dtype=F32) / half)
    ang = pos.astype(F32)[:, None] * inv_freq[None, :]
    cos, sin = jnp.cos(ang), jnp.sin(ang)
    n = pos.shape[0]
    one = jnp.ones((n, HEAD_DIM - ROPE_DIM), F32)
    zero = jnp.zeros((n, HEAD_DIM - ROPE_DIM), F32)
    z8 = jnp.zeros((n, half), F32)
    c = jnp.concatenate([cos, cos, one], axis=1)
    sa = jnp.concatenate([z8, sin, zero], axis=1)
    sb = jnp.concatenate([-sin, z8, zero], axis=1)
    return tuple(jnp.tile(t, (1, 2)) for t in (c, sa, sb))


def _masked_query_rows(q, n_seq, n_new, n_groups):
    w = q.shape[1]
    q = q.reshape(n_seq, 1, n_new, w)
    q = jnp.pad(q, ((0, 0), (0, 0), (0, SUB - n_new), (0, 0)))
    grp = (jnp.arange(w) // HEAD_DIM)[None, :] == jnp.arange(n_groups)[:, None]
    q = jnp.where(grp[None, :, None, :], q, jnp.zeros((), q.dtype))
    return q.reshape(n_seq, n_groups * SUB, w)


def _pad_keys(x, n_seq, n_new):
    x = x.reshape(n_seq, n_new, x.shape[1])
    return jnp.pad(x, ((0, 0), (0, PAGE_SIZE - n_new), (0, 0)))


def _pad_keys_t(x, n_seq, n_new):
    x = jnp.swapaxes(x.reshape(n_seq, n_new, x.shape[1]), 1, 2)
    return jnp.pad(x, ((0, 0), (0, 0), (0, PAGE_SIZE - n_new)))


def _layer_params(l, a):
    bf = lambda x: x.astype(BF16)
    row = lambda x: x.reshape(1, -1)
    d = a["norm_mix"].shape[1]
    fox_bias = jnp.zeros((1, MISC_W), F32).at[0, :N_HEADS].set(a["fox_bias_f"][l])
    blk = jnp.arange(HW) // HEAD_DIM
    bd = jnp.where(blk[:, None] == blk[None, :], 1.0 / HEAD_DIM, 0.0).astype(BF16)
    return {
        "norm_ffn1": row(a["norm_ffn1"][l]), "norm_mix": row(a["norm_mix"][l]), "norm_ffn2": row(a["norm_ffn2"][l]),
        "ffn1": (bf(a["ffn1_w_gate"][l]), bf(a["ffn1_w_up"][l]), bf(a["ffn1_w_down"][l])),
        "ffn2": (bf(a["ffn2_w_gate"][l]), bf(a["ffn2_w_up"][l]), bf(a["ffn2_w_down"][l])),
        "w_in": bf(_pack_w_in(a["w_in"][l])), "fox_bias": fox_bias, "bd": bd,
        "gaq": row(jnp.tile(a["a_q_norm"][l], N_HEADS)), "gak": row(jnp.tile(a["a_k_norm"][l], N_HEADS)),
        "gbq": row(jnp.tile(a["b_q_norm"][l], N_HEADS)), "gbk": row(jnp.tile(a["b_k_norm"][l], N_HEADS)),
        "gcq": row(jnp.tile(a["c_q_norm"][l].reshape(-1), N_HEADS)),
        "gck": row(jnp.tile(a["c_k_norm"][l].reshape(-1), N_HEADS)),
        "lq1": row(a["c_lambda_q1"][l]), "lk1": row(a["c_lambda_k1"][l]),
        "lq2": row(a["c_lambda_q2"][l]), "lk2": row(a["c_lambda_k2"][l]),
        "c_subln": row(a["c_subln"][l]),
        "w_gate": bf(a["w_gate"][l]), "b_gate": row(a["b_gate"][l]),
        "w_o_a": bf(a["w_o_a"][l]), "w_o_b": bf(a["w_o_b"][l]), "w_o_c": bf(a["w_o_c"][l]),
        "w_out": bf(a["w_out"][l]),
    }


_INPROJ_NAMES = ("aq", "akf", "akb", "avf", "avb", "bq", "bkf", "bkb", "bvf", "bvb", "biq", "bif", "bib",
                 "cq", "ckf", "ckb", "cvf", "cvb", "misc", "cum")


def kernel(x_prompt, x_sample, cache_a_k, cache_a_v, cache_a_logf, cache_b_k, cache_b_v, cache_b_idx, cache_c_k, cache_c_v, page_table, norm_ffn1, ffn1_w_gate, ffn1_w_up, ffn1_w_down, norm_mix, w_in, fox_bias_f, a_q_norm, a_k_norm, b_q_norm, b_k_norm, c_q_norm, c_k_norm, c_lambda_q1, c_lambda_k1, c_lambda_q2, c_lambda_k2, c_subln, w_o_a, w_o_b, w_o_c, w_gate, b_gate, w_out, norm_ffn2, ffn2_w_gate, ffn2_w_up, ffn2_w_down):
    a = dict(norm_ffn1=norm_ffn1, ffn1_w_gate=ffn1_w_gate, ffn1_w_up=ffn1_w_up, ffn1_w_down=ffn1_w_down,
             norm_mix=norm_mix, w_in=w_in, fox_bias_f=fox_bias_f, a_q_norm=a_q_norm, a_k_norm=a_k_norm,
             b_q_norm=b_q_norm, b_k_norm=b_k_norm, c_q_norm=c_q_norm, c_k_norm=c_k_norm,
             c_lambda_q1=c_lambda_q1, c_lambda_k1=c_lambda_k1, c_lambda_q2=c_lambda_q2, c_lambda_k2=c_lambda_k2,
             c_subln=c_subln, w_o_a=w_o_a, w_o_b=w_o_b, w_o_c=w_o_c, w_gate=w_gate, b_gate=b_gate, w_out=w_out,
             norm_ffn2=norm_ffn2, ffn2_w_gate=ffn2_w_gate, ffn2_w_up=ffn2_w_up, ffn2_w_down=ffn2_w_down)
    depth = w_in.shape[0]
    bsz, seq, d = x_prompt.shape
    nb, n_new, _ = x_sample.shape
    n_pool = cache_a_k.shape[1]
    npg = page_table.shape[1]
    past_len = npg * PAGE_SIZE
    np_tok, ns_tok = bsz * seq, nb * n_new
    tm_p = min(512, seq)
    tm_s = min(512, ns_tok)
    tq = min(256, seq)
    tk = min(512, seq)

    assert n_new <= SUB and np_tok % tm_p == 0 and ns_tok % tm_s == 0 and tm_s % n_new == 0
    keys_last = lambda c, w: jnp.moveaxis(c, 2, -1).reshape(depth, n_pool, w, PAGE_SIZE)
    caches = (keys_last(cache_a_k, HW), keys_last(cache_a_v, HW), keys_last(cache_a_logf, N_HEADS),
              keys_last(cache_b_k, HW), keys_last(cache_b_v, HW), keys_last(cache_c_k, CW),
              cache_c_v.reshape(depth, n_pool, PAGE_SIZE * N_HEADS, 2 * HEAD_DIM))
    idx_t = keys_last(cache_b_idx, HEAD_DIM)
    pt = page_table.reshape(-1).astype(jnp.int32)
    pg = next(g for g in (8, 4, 2, 1) if npg % g == 0)
    tk_total = past_len + n_new
    ncols = -(-(npg + 1) * PAGE_SIZE // COUNT_W) * COUNT_W
    topk_s = min(TOPK_MAX, tk_total // 4)

    tabs_p = _rope_tables(jnp.arange(seq, dtype=jnp.int32))
    pos_s = past_len + jnp.arange(n_new, dtype=jnp.int32)
    tabs_s = _rope_tables(jnp.tile(pos_s, tm_s // n_new))

    xp = x_prompt.reshape(np_tok, d)
    xs = x_sample.reshape(ns_tok, d)
    rows_p, rows_s = [], []
    for l in range(depth):
        p = _layer_params(l, a)
        lam_init = 0.8 - 0.6 * math.exp(-0.3 * l)

        xp = _ffn(xp, p["norm_ffn1"], *p["ffn1"], tm_p)
        pr = dict(zip(_INPROJ_NAMES, _inproj(xp, p, tabs_p, tm_p, seq)))
        b3 = lambda z: z.reshape(bsz, seq, z.shape[1])
        b3t = lambda z: jnp.swapaxes(b3(z), 1, 2)
        kcum = jnp.swapaxes(b3(pr["cum"])[:, :, :SUB], 1, 2)
        oa = _fox_prompt(b3(pr["aq"]), b3(pr["akb"]), b3(pr["avb"]), kcum, tq, tk)
        iw_t = jnp.swapaxes(b3(pr["misc"])[:, :, N_HEADS:N_HEADS + SUB], 1, 2)
        ob = _dsa_prompt(b3(pr["bq"]), b3(pr["biq"]), iw_t, b3(pr["bkb"]), b3(pr["bvb"]), b3(pr["bib"]), tq, tk)
        oc = _diff_prompt(b3(pr["cq"]), b3(pr["ckb"]), b3t(pr["cvb"]), p, lam_init, tq, tk)
        xp = _post(xp, oa.reshape(np_tok, HW), ob.reshape(np_tok, HW), oc.reshape(np_tok, CW), p, tm_p)
        xp = _ffn(xp, p["norm_ffn2"], *p["ffn2"], tm_p)
        rows_p.append(pr)

        xs = _ffn(xs, p["norm_ffn1"], *p["ffn1"], tm_s)
        sr = dict(zip(_INPROJ_NAMES, _inproj(xs, p, tabs_s, tm_s, ns_tok)))
        iq = jnp.swapaxes(sr["biq"].reshape(nb, n_new, N_HEADS, HEAD_DIM), 1, 2)
        iq_rows = jnp.pad(iq, ((0, 0), (0, 0), (0, SUB - n_new), (0, 0))).reshape(nb, N_HEADS * SUB, HEAD_DIM)
        iw = sr["misc"][:, N_HEADS:2 * N_HEADS].reshape(nb, n_new, N_HEADS)
        iw = jnp.pad(jnp.swapaxes(iw, 1, 2), ((0, 0), (0, 0), (0, SUB - n_new)))
        iw_rows = iw.reshape(nb, N_HEADS * SUB, 1)
        new_idx_t = _pad_keys_t(sr["bib"][:, :HEAD_DIM], nb, n_new)
        keys = _sample_scores(pt, iq_rows, iw_rows, idx_t, new_idx_t, l, ncols)
        bias = _sample_select(keys.reshape(nb * SUB, ncols).T, topk_s).T.reshape(nb, SUB, ncols)
        q_rows = (_masked_query_rows(sr["aq"], nb, n_new, N_HEADS),
                  _masked_query_rows(sr["bq"], nb, n_new, N_HEADS),
                  _masked_query_rows(sr["cq"], nb, n_new, 2 * N_HEADS))
        new_rows = (_pad_keys_t(sr["akb"], nb, n_new), _pad_keys_t(sr["avb"], nb, n_new),
                    _pad_keys_t(sr["misc"][:, :N_HEADS], nb, n_new),
                    _pad_keys_t(sr["bkb"], nb, n_new), _pad_keys_t(sr["bvb"], nb, n_new),
                    _pad_keys_t(sr["ckb"], nb, n_new), _pad_keys(sr["cvb"], nb, n_new))
        soa, sob, soc = _sample_attention(pt, q_rows, bias, caches, new_rows, p, l, lam_init, pg)
        take = lambda o: o[:, :n_new].reshape(ns_tok, o.shape[2])
        xs = _post(xs, take(soa), take(sob), take(soc), p, tm_s)
        xs = _ffn(xs, p["norm_ffn2"], *p["ffn2"], tm_s)
        rows_s.append(sr)

    def stack(rows, name, lead, tail):
        z = jnp.stack([r[name] for r in rows])
        if name == "misc":
            z = z[:, :, :N_HEADS]
        return z.reshape((depth,) + lead + tail)

    lead_p, lead_s = (bsz, seq), (nb, n_new)
    hd = (N_HEADS, HEAD_DIM)
    out = [xp.reshape(bsz, seq, d), xs.reshape(nb, n_new, d)]
    for name, tail in (("akf", hd), ("avf", hd), ("misc", (N_HEADS,)), ("bkf", hd), ("bvf", hd),
                       ("bif", (HEAD_DIM,)), ("ckf", (N_HEADS, 2, HEAD_DIM)), ("cvf", (N_HEADS, 2 * HEAD_DIM))):
        out.append(stack(rows_p, name, lead_p, tail))
        out.append(stack(rows_s, name, lead_s, tail))
    return tuple(out)
```

```python
import functools
import math

import jax
import jax.numpy as jnp
from jax import lax
from jax.experimental import pallas as pl
from jax.experimental.pallas import tpu as pltpu

F32 = jnp.float32
BF16 = jnp.bfloat16

HEAD_DIM = 64
ROPE_DIM = HEAD_DIM // 4
ROPE_THETA = 500000.0
NORM_EPS = 1e-6
TOPK_MAX = 256
PAGE_SIZE = 128
N_HEADS = 4
HW = N_HEADS * HEAD_DIM
CW = 2 * HW
MISC_W = 128
NEG = -1e30
SUB = 8
VMEM_LIMIT = 56 * 1024 * 1024


def _cparams(*sem):
    return pltpu.CompilerParams(dimension_semantics=sem, vmem_limit_bytes=VMEM_LIMIT)


def _const_spec(shape):
    nd = len(shape)
    return pl.BlockSpec(shape, lambda *_: (0,) * nd, pipeline_mode=pl.Buffered(1))


def _rms(x, g):
    ms = jnp.mean(x * x, axis=-1, keepdims=True)
    return x * lax.rsqrt(ms + NORM_EPS) * g


def _sigmoid(x):
    return 1.0 / (1.0 + jnp.exp(-x))


def _lane_head(shape, width=HEAD_DIM):
    return lax.broadcasted_iota(jnp.int32, shape, len(shape) - 1) // width


def _ffn_body(x_ref, g_ref, wg_ref, wu_ref, wd_ref, o_ref, acc_ref, *, fc):
    x = x_ref[...]
    h = _rms(x, g_ref[...]).astype(BF16)
    dff = wg_ref.shape[1]
    for c in range(dff // fc):
        sl = slice(c * fc, (c + 1) * fc)
        g = jnp.dot(h, wg_ref[:, sl], preferred_element_type=F32)
        u = jnp.dot(h, wu_ref[:, sl], preferred_element_type=F32)
        a = (g * _sigmoid(g) * u).astype(BF16)
        d = jnp.dot(a, wd_ref[sl, :], preferred_element_type=F32)
        if c == 0:
            acc_ref[...] = d
        else:
            acc_ref[...] += d
    o_ref[...] = x + 0.5 * acc_ref[...]


def _ffn(x, g, wg, wu, wd, tm):
    n, d = x.shape
    dff = wg.shape[1]
    return pl.pallas_call(
        functools.partial(_ffn_body, fc=256),
        grid=(n // tm,),
        in_specs=[pl.BlockSpec((tm, d), lambda i: (i, 0)),
                  _const_spec((1, d)), _const_spec((d, dff)), _const_spec((d, dff)), _const_spec((dff, d))],
        out_specs=pl.BlockSpec((tm, d), lambda i: (i, 0)),
        out_shape=jax.ShapeDtypeStruct((n, d), F32),
        scratch_shapes=[pltpu.VMEM((tm, d), F32)],
        compiler_params=_cparams("arbitrary"),
        name="ffn",
    )(x, g, wg, wu, wd)


_O_AQ, _O_AK, _O_AV = 0, HW, 2 * HW
_O_BQ, _O_BK, _O_BV, _O_BIQ, _O_BIK = 3 * HW, 4 * HW, 5 * HW, 6 * HW, 7 * HW
_O_CQ, _O_CK, _O_CV = 8 * HW, 8 * HW + CW, 8 * HW + 2 * CW
_O_MISC = 8 * HW + 3 * CW
PACK_W = _O_MISC + MISC_W


def _pack_w_in(w):
    widths = (HW, HW, HW, N_HEADS, HW, HW, HW, HW, HEAD_DIM, N_HEADS, CW, CW, CW)
    offs = [0]
    for wd in widths:
        offs.append(offs[-1] + wd)
    part = lambda i: w[:, offs[i]:offs[i + 1]]
    aq, ak, av, af, bq, bk, bv, biq, bik, biw, cq, ck, cv = [part(i) for i in range(13)]
    misc = jnp.concatenate([af, biw, jnp.zeros((w.shape[0], MISC_W - 2 * N_HEADS), w.dtype)], axis=1)
    return jnp.concatenate([aq, ak, av, bq, bk, bv, biq, jnp.tile(bik, (1, N_HEADS)), cq, ck, cv, misc], axis=1)


def _headnorm(x, bd_ref, g):
    sq = x * x
    hi = sq.astype(BF16)
    lo = (sq - hi.astype(F32)).astype(BF16)
    ms = (jnp.dot(hi, bd_ref[...], preferred_element_type=F32)
          + jnp.dot(lo, bd_ref[...], preferred_element_type=F32))
    return x * lax.rsqrt(ms + NORM_EPS) * g


def _rope(x, cos, sa, sb):
    w = x.shape[1]
    rep = w // cos.shape[1]
    if rep > 1:
        cos, sa, sb = (jnp.concatenate([t] * rep, axis=1) for t in (cos, sa, sb))
    half = ROPE_DIM // 2
    return x * cos + pltpu.roll(x, half, axis=1) * sa + pltpu.roll(x, w - half, axis=1) * sb


def _inproj_body(x_ref, g_ref, w_ref, fb_ref, gaq_ref, gak_ref, gbq_ref, gbk_ref, gcq_ref, gck_ref,
                 cos_ref, sa_ref, sb_ref, bd_ref,
                 aq_o, akf_o, akb_o, avf_o, avb_o, bq_o, bkf_o, bkb_o, bvf_o, bvb_o, biq_o, bif_o, bib_o,
                 cq_o, ckf_o, ckb_o, cvf_o, cvb_o, misc_o, cum_o, carry_ref, *, tiles_per_seq, scale):
    i = pl.program_id(0)
    x = x_ref[...]
    h = _rms(x, g_ref[...]).astype(BF16)
    proj = lambda off, wd: jnp.dot(h, w_ref[:, off:off + wd], preferred_element_type=F32)
    cos, sa, sb = cos_ref[...], sa_ref[...], sb_ref[...]

    aq = _headnorm(proj(_O_AQ, HW), bd_ref, gaq_ref[...])
    aq_o[...] = (aq * scale).astype(BF16)
    ak = _headnorm(proj(_O_AK, HW), bd_ref, gak_ref[...])
    akf_o[0] = ak.T
    akb_o[...] = ak.astype(BF16)
    av = proj(_O_AV, HW)
    avf_o[0] = av.T
    avb_o[...] = av.astype(BF16)

    bq = _rope(_headnorm(proj(_O_BQ, HW), bd_ref, gbq_ref[...]), cos, sa, sb)
    bq_o[...] = (bq * scale).astype(BF16)
    bk = _rope(_headnorm(proj(_O_BK, HW), bd_ref, gbk_ref[...]), cos, sa, sb)
    bkf_o[0] = bk.T
    bkb_o[...] = bk.astype(BF16)
    bv = proj(_O_BV, HW)
    bvf_o[0] = bv.T
    bvb_o[...] = bv.astype(BF16)
    biq_o[...] = _rope(proj(_O_BIQ, HW), cos, sa, sb).astype(BF16)
    bik = _rope(proj(_O_BIK, HW), cos, sa, sb)
    bif_o[0] = bik.T[:HEAD_DIM]
    bib_o[...] = bik.astype(BF16)

    for half in range(2):
        sl = slice(half * HW, (half + 1) * HW)
        cq = _rope(_headnorm(proj(_O_CQ + half * HW, HW), bd_ref, gcq_ref[:, sl]), cos, sa, sb)
        cq_o[:, sl] = (cq * scale).astype(BF16)
        ck = _rope(_headnorm(proj(_O_CK + half * HW, HW), bd_ref, gck_ref[:, sl]), cos, sa, sb)
        ckf_o[0, sl, :] = ck.T
        ckb_o[:, sl] = ck.astype(BF16)
    cv = proj(_O_CV, CW)
    for head in range(N_HEADS):
        cvf_o[pl.ds(head, cv.shape[0], stride=N_HEADS), :] = cv[:, head * 2 * HEAD_DIM:(head + 1) * 2 * HEAD_DIM]
    cvb_o[...] = cv.astype(BF16)

    misc = proj(_O_MISC, MISC_W)
    z = misc + fb_ref[...]
    logf = jnp.minimum(z, 0.0) - jnp.log(1.0 + jnp.exp(-jnp.abs(z)))
    lane = lax.broadcasted_iota(jnp.int32, misc.shape, 1)
    misc = jnp.where(lane < N_HEADS, logf, misc)
    misc_o[...] = misc

    tm = misc.shape[0]
    r = lax.broadcasted_iota(jnp.int32, (tm, tm), 0)
    c = lax.broadcasted_iota(jnp.int32, (tm, tm), 1)
    tri = jnp.where(c <= r, 1.0, 0.0).astype(F32)

    @pl.when(i % tiles_per_seq == 0)
    def _():
        carry_ref[...] = jnp.zeros_like(carry_ref)

    cum = jnp.dot(tri, misc, preferred_element_type=F32, precision=lax.Precision.HIGHEST) + carry_ref[...]
    cum_o[...] = cum
    carry_ref[...] = cum[tm - 1:tm, :]


def _inproj(x, p, tabs, tm, seq_len):
    n, d = x.shape
    cos, sa, sb = tabs
    tab_tiles = cos.shape[0] // tm
    tps = max(seq_len // tm, 1)
    nseq = n // (tps * tm)
    row = lambda w: pl.BlockSpec((tm, w), lambda i: (i, 0))
    tab = pl.BlockSpec((tm, cos.shape[1]), lambda i: (i % tab_tiles, 0))
    tok = lambda w, dt: (row(w), jax.ShapeDtypeStruct((n, w), dt))
    feat = lambda w: (pl.BlockSpec((1, w, tm), lambda i: (i // tps, 0, i % tps)),
                      jax.ShapeDtypeStruct((nseq, w, tps * tm), F32))
    cv_rows = (pl.BlockSpec((tm * N_HEADS, 2 * HEAD_DIM), lambda i: (i, 0)),
               jax.ShapeDtypeStruct((n * N_HEADS, 2 * HEAD_DIM), F32))
    outs = [
        tok(HW, BF16),
        feat(HW), tok(HW, BF16), feat(HW), tok(HW, BF16),
        tok(HW, BF16),
        feat(HW), tok(HW, BF16), feat(HW), tok(HW, BF16),
        tok(HW, BF16),
        feat(HEAD_DIM), tok(HW, BF16),
        tok(CW, BF16),
        feat(CW), tok(CW, BF16), cv_rows, tok(CW, BF16),
        tok(MISC_W, F32), tok(MISC_W, F32),
    ]
    return pl.pallas_call(
        functools.partial(_inproj_body, tiles_per_seq=tps, scale=HEAD_DIM ** -0.5),
        grid=(n // tm,),
        in_specs=[row(d), _const_spec((1, d)), _const_spec((d, PACK_W)), _const_spec((1, MISC_W)),
                  _const_spec((1, HW)), _const_spec((1, HW)), _const_spec((1, HW)), _const_spec((1, HW)),
                  _const_spec((1, CW)), _const_spec((1, CW)), tab, tab, tab, _const_spec((HW, HW))],
        out_specs=[spec for spec, _ in outs],
        out_shape=[shape for _, shape in outs],
        scratch_shapes=[pltpu.VMEM((1, MISC_W), F32)],
        compiler_params=_cparams("arbitrary"),
        name="inproj",
    )(x, p["norm_mix"], p["w_in"], p["fox_bias"], p["gaq"], p["gak"], p["gbq"], p["gbk"], p["gcq"], p["gck"],
      cos, sa, sb, p["bd"])


def _online_update(s, m_prev, l_prev):
    m_new = jnp.maximum(m_prev, jnp.max(s, axis=-1, keepdims=True))
    alpha = jnp.exp(m_prev - m_new)
    p = jnp.exp(s - m_new)
    l_new = alpha * l_prev + jnp.sum(p, axis=-1, keepdims=True)
    return p, m_new, l_new, alpha


def _qk(q, k):
    return lax.dot_general(q, k, (((1,), (1,)), ((), ())), preferred_element_type=F32)


def _head_masks(dtype):
    return [(_lane_head((1, HW)) == h).astype(dtype) for h in range(N_HEADS)]


def _online_update_t(s, m_prev, l_prev):
    m_new = jnp.maximum(m_prev, jnp.max(s, axis=0, keepdims=True))
    alpha = jnp.exp(m_prev - m_new)
    p = jnp.exp(s - m_new)
    l_new = alpha * l_prev + jnp.sum(p, axis=0, keepdims=True)
    return p, m_new, l_new, alpha


def _softmax_state_init(m_ref, l_ref, acc_ref):
    m_ref[...] = jnp.full(m_ref.shape, NEG, F32)
    l_ref[...] = jnp.zeros(l_ref.shape, F32)
    acc_ref[...] = jnp.zeros(acc_ref.shape, F32)


def _causal_tiles(i, tq, tk):
    qpos = i * tq + lax.broadcasted_iota(jnp.int32, (tk, tq), 1)
    kiota = lax.broadcasted_iota(jnp.int32, (tk, tq), 0)
    return qpos, kiota, (i * tq + tq + tk - 1) // tk


def _by_head(cols):
    lh = _lane_head((cols[0].shape[0], HW))
    out = cols[N_HEADS - 1]
    for h in range(N_HEADS - 2, -1, -1):
        out = jnp.where(lh == h, cols[h], out)
    return out


def _heads_init(q, qs_ref, m_ref, l_ref, acc_ref):
    for h, hm in enumerate(_head_masks(q.dtype)):
        qs_ref[h] = q * hm
    _softmax_state_init(m_ref, l_ref, acc_ref)


def _heads_chunk(qs_ref, k, v, logit_fn, m_ref, l_ref, acc_ref):
    ps, alphas = [], []
    for h in range(N_HEADS):
        s = logit_fn(h, _qk(qs_ref[h], k))
        p, m_new, l_new, alpha = _online_update(s, m_ref[h], l_ref[h])
        m_ref[h] = m_new
        l_ref[h] = l_new
        ps.append(p.astype(BF16))
        alphas.append(alpha)
    p_cat = jnp.concatenate(ps, axis=1)
    v_bd = jnp.concatenate([v * hm for hm in _head_masks(v.dtype)], axis=0)
    acc_ref[...] = _by_head(alphas) * acc_ref[...] + jnp.dot(p_cat, v_bd, preferred_element_type=F32)


def _heads_finish(l_ref, acc_ref):
    return acc_ref[...] / _by_head([l_ref[h] for h in range(N_HEADS)])


def _fox_body(q_ref, k_ref, v_ref, kc_ref, o_ref, qs_ref, m_ref, l_ref, acc_ref, *, tq, tk):
    i = pl.program_id(1)
    _heads_init(q_ref[0], qs_ref, m_ref, l_ref, acc_ref)
    qpos = i * tq + lax.broadcasted_iota(jnp.int32, (tq, tk), 0)
    kiota = lax.broadcasted_iota(jnp.int32, (tq, tk), 1)
    nk = (i * tq + tq + tk - 1) // tk

    def chunk(j, carry):
        off = pl.multiple_of(j * tk, tk)
        visible = (kiota + off) <= qpos
        logit = lambda h, s: jnp.where(visible, s - kc_ref[0, h:h + 1, pl.ds(off, tk)], NEG)
        _heads_chunk(qs_ref, k_ref[0, pl.ds(off, tk), :], v_ref[0, pl.ds(off, tk), :], logit,
                     m_ref, l_ref, acc_ref)
        return carry

    lax.fori_loop(0, nk, chunk, 0)
    o_ref[0] = _heads_finish(l_ref, acc_ref).astype(BF16)


def _heads_specs(b, t, w, tq):
    tile = pl.BlockSpec((1, tq, w), lambda bi, i: (bi, i, 0))
    keys = pl.BlockSpec((1, t, w), lambda bi, i: (bi, 0, 0))
    scratch = [pltpu.VMEM((N_HEADS, tq, w), BF16), pltpu.VMEM((N_HEADS, tq, 1), F32),
               pltpu.VMEM((N_HEADS, tq, 1), F32), pltpu.VMEM((tq, w), F32)]
    return tile, keys, scratch


def _fox_prompt(q, k, v, kcum, tq, tk):
    b, t, w = q.shape
    tile, keys, scratch = _heads_specs(b, t, w, tq)
    return pl.pallas_call(
        functools.partial(_fox_body, tq=tq, tk=tk),
        grid=(b, t // tq),
        in_specs=[tile, keys, keys, pl.BlockSpec((1, SUB, t), lambda bi, i: (bi, 0, 0))],
        out_specs=tile,
        out_shape=jax.ShapeDtypeStruct((b, t, w), BF16),
        scratch_shapes=scratch,
        compiler_params=_cparams("arbitrary", "arbitrary"),
        name="fox_prompt",
    )(q, k, v, kcum)


def _diff_finish(acc1, l1, acc2, l2, lam, gain, out_scale):
    o = acc1 / l1 - lam * (acc2 / l2)
    return _rms(o, gain) * out_scale


def _lambda(lq1, lk1, lq2, lk2, lam_init):
    return (jnp.exp(jnp.sum(lq1 * lk1, axis=-1, keepdims=True))
            - jnp.exp(jnp.sum(lq2 * lk2, axis=-1, keepdims=True)) + lam_init)


def _diff_body(q_ref, k_ref, vt_ref, lq1_ref, lk1_ref, lq2_ref, lk2_ref, sgt_ref, o_ref,
               qs_ref, m_ref, l_ref, acc_ref, *, tq, tk, lam_init):
    nmap = 2 * N_HEADS
    for half in range(2):
        q = q_ref[0, :, half * HW:(half + 1) * HW]
        for g, hm in enumerate(_head_masks(q.dtype)):
            qs_ref[half * N_HEADS + g] = q * hm
    _softmax_state_init(m_ref, l_ref, acc_ref)
    qpos, kiota, nk = _causal_tiles(pl.program_id(1), tq, tk)
    vw = 2 * HEAD_DIM

    def chunk(j, carry):
        off = pl.multiple_of(j * tk, tk)
        visible = (kiota + off) <= qpos
        for mp in range(nmap):
            half, head = mp // N_HEADS, mp // 2
            k = k_ref[0, pl.ds(off, tk), half * HW:(half + 1) * HW]
            vt = vt_ref[0, head * vw:(head + 1) * vw, pl.ds(off, tk)]
            s = jnp.where(visible, _qk(k, qs_ref[mp]), NEG)
            p, m_new, l_new, alpha = _online_update_t(s, m_ref[mp], l_ref[mp])
            m_ref[mp] = m_new
            l_ref[mp] = l_new
            acc_ref[mp] = alpha * acc_ref[mp] + jnp.dot(vt, p.astype(BF16), preferred_element_type=F32)
        return carry

    lax.fori_loop(0, nk, chunk, 0)
    lam = _lambda(lq1_ref[...], lk1_ref[...], lq2_ref[...], lk2_ref[...], lam_init)
    outs = []
    for head in range(N_HEADS):
        o = acc_ref[2 * head] / l_ref[2 * head] - lam * (acc_ref[2 * head + 1] / l_ref[2 * head + 1])
        ms = jnp.mean(o * o, axis=0, keepdims=True)
        outs.append(o * lax.rsqrt(ms + NORM_EPS) * sgt_ref[...] * (1.0 - lam_init))
    o_ref[0] = jnp.concatenate(outs, axis=0).T.astype(BF16)


def _diff_prompt(q, k, vt, p, lam_init, tq, tk):
    b, t, w = q.shape
    nmap = 2 * N_HEADS
    vec = _const_spec((1, HEAD_DIM))
    tile = pl.BlockSpec((1, tq, w), lambda bi, i: (bi, i, 0))
    return pl.pallas_call(
        functools.partial(_diff_body, tq=tq, tk=tk, lam_init=lam_init),
        grid=(b, t // tq),
        in_specs=[tile,
                  pl.BlockSpec((1, t, w), lambda bi, i: (bi, 0, 0)),
                  pl.BlockSpec((1, w, t), lambda bi, i: (bi, 0, 0)),
                  vec, vec, vec, vec, _const_spec((2 * HEAD_DIM, 1))],
        out_specs=tile,
        out_shape=jax.ShapeDtypeStruct((b, t, w), BF16),
        scratch_shapes=[pltpu.VMEM((nmap, tq, HW), BF16), pltpu.VMEM((nmap, 1, tq), F32),
                        pltpu.VMEM((nmap, 1, tq), F32), pltpu.VMEM((nmap, 2 * HEAD_DIM, tq), F32)],
        compiler_params=_cparams("arbitrary", "arbitrary"),
        name="diff_prompt",
    )(q, k, vt, p["lq1"], p["lk1"], p["lq2"], p["lk2"], p["c_subln"].reshape(-1, 1))


INT_MIN = -2 ** 31
KEY_NEG_INF = int(0xFF800000 - 2 ** 32) ^ 0x7FFFFFFF


def _sort_key(score):
    bits = lax.bitcast_convert_type(score, jnp.int32)
    return jnp.where(bits < 0, bits ^ jnp.int32(0x7FFFFFFF), bits)


COUNT_UNROLL = 4
COUNT_W = COUNT_UNROLL * 128


def _count_keys(keys_ref, n_iter, hit, operands):
    nq = keys_ref.shape[1]
    pos = lax.broadcasted_iota(jnp.int32, (128, nq), 0)

    def chunk(c, acc):
        for u in range(COUNT_UNROLL):
            off = pl.multiple_of(c * COUNT_W + u * 128, 128)
            h = hit(keys_ref[pl.ds(off, 128), :], pos + off, *operands)
            acc = acc + sum(h[r:r + SUB] for r in range(0, 128, SUB))
        return acc

    acc = lax.fori_loop(0, n_iter, chunk, jnp.zeros((SUB, nq), F32))
    return jnp.sum(acc, axis=0, keepdims=True)


def _topk_select(keys_ref, n_iter, topk, pos_bits):
    nq = keys_ref.shape[1]
    kf = float(topk)
    one = lambda m: jnp.where(m, 1.0, 0.0)
    count = functools.partial(_count_keys, keys_ref, n_iter)
    cnt0 = count(lambda kk, pos: one(kk >= 0), [])
    thr0 = jnp.where(cnt0 >= kf, jnp.int32(0), jnp.int32(INT_MIN))

    def bit_step(t, thr):
        cand = thr | jnp.left_shift(jnp.int32(1), 30 - t)
        cnt = count(lambda kk, pos, cd: one(kk >= cd), [cand])
        return jnp.where(cnt >= kf, cand, thr)

    thr = lax.fori_loop(0, 31, bit_step, thr0)
    n_gt = count(lambda kk, pos, th: one(kk > th), [thr])
    n_ge = count(lambda kk, pos, th: one(kk >= th), [thr])
    need = kf - n_gt
    excess = jnp.where(n_ge > kf, one(thr > KEY_NEG_INF), 0.0)
    all_pos = jnp.full((1, nq), 2 ** pos_bits - 1, jnp.int32)

    def tie_search():
        def pos_step(t, x):
            cand = x | jnp.left_shift(jnp.int32(1), pos_bits - 1 - t)
            cnt = count(lambda kk, pos, th, cd: jnp.where(kk == th, one(pos < cd), 0.0), [thr, cand])
            return jnp.where(cnt < need, cand, x)
        return lax.fori_loop(0, pos_bits, pos_step, jnp.zeros((1, nq), jnp.int32))

    xpos = lax.cond(jnp.max(excess) > 0.0, tie_search, lambda: all_pos)
    return thr, xpos


def _select_bias(kk, pos, thr, xpos):
    tie = jnp.where(pos <= xpos, 0.0, NEG)
    sel = jnp.where(kk > thr, 0.0, jnp.where(kk == thr, tie, NEG))
    return jnp.where(kk > KEY_NEG_INF, sel, NEG)


def _dsa_body(q_ref, iq_ref, iw_ref, k_ref, v_ref, ik_ref, o_ref,
              qs_ref, iqs_ref, keys_ref, m_ref, l_ref, acc_ref, *, tq, tk, topk, pos_bits):
    iq = iq_ref[0]
    for h, hm in enumerate(_head_masks(iq.dtype)):
        iqs_ref[h] = iq * hm
    qpos, kiota, nk = _causal_tiles(pl.program_id(1), tq, tk)
    iw = iw_ref[0]

    def score_chunk(j, carry):
        off = pl.multiple_of(j * tk, tk)
        ik = ik_ref[0, pl.ds(off, tk), :]
        score = jnp.zeros((tk, tq), F32)
        for h in range(N_HEADS):
            score = score + iw[h:h + 1, :] * jnp.maximum(_qk(ik, iqs_ref[h]), 0.0)
        score = jnp.where((kiota + off) <= qpos, score, -jnp.inf)
        keys_ref[pl.ds(off, tk), :] = _sort_key(score)
        return carry

    lax.fori_loop(0, nk, score_chunk, 0)
    thr, xpos = _topk_select(keys_ref, nk * (tk // COUNT_W), topk, pos_bits)

    _heads_init(q_ref[0], qs_ref, m_ref, l_ref, acc_ref)

    def chunk(j, carry):
        off = pl.multiple_of(j * tk, tk)
        bias = _select_bias(keys_ref[pl.ds(off, tk), :], kiota + off, thr, xpos).T
        _heads_chunk(qs_ref, k_ref[0, pl.ds(off, tk), :], v_ref[0, pl.ds(off, tk), :],
                     lambda h, s: s + bias, m_ref, l_ref, acc_ref)
        return carry

    lax.fori_loop(0, nk, chunk, 0)
    o_ref[0] = _heads_finish(l_ref, acc_ref).astype(BF16)


def _dsa_prompt(q, iq, iw_t, k, v, ik, tq, tk):
    b, t, w = q.shape
    topk = min(TOPK_MAX, t // 4)
    assert tk >= topk and tk % COUNT_W == 0
    tile, keys, scratch = _heads_specs(b, t, w, tq)
    return pl.pallas_call(
        functools.partial(_dsa_body, tq=tq, tk=tk, topk=topk, pos_bits=max(t - 1, 1).bit_length()),
        grid=(b, t // tq),
        in_specs=[tile, tile, pl.BlockSpec((1, SUB, tq), lambda bi, i: (bi, 0, i)), keys, keys, keys],
        out_specs=tile,
        out_shape=jax.ShapeDtypeStruct((b, t, w), BF16),
        scratch_shapes=[scratch[0], pltpu.VMEM((N_HEADS, tq, w), BF16), pltpu.VMEM((t, tq), jnp.int32)] + scratch[1:],
        compiler_params=_cparams("arbitrary", "arbitrary"),
        name="dsa_prompt",
    )(q, iq, iw_t, k, v, ik)


def _post_body(x_ref, oa_ref, ob_ref, oc_ref, g_ref, wgate_ref, bgate_ref, woa_ref, wob_ref, woc_ref, wout_ref,
               o_ref, acc_ref, *, mc):
    x = x_ref[...]
    d = x.shape[1]
    h = _rms(x, g_ref[...]).astype(BF16)
    oa, ob, oc = oa_ref[...], ob_ref[...], oc_ref[...]
    for c in range(d // mc):
        merged = None
        for br, (o, w_ref) in enumerate(((oa, woa_ref), (ob, wob_ref), (oc, woc_ref))):
            sl = slice(br * d + c * mc, br * d + (c + 1) * mc)
            gate = _sigmoid(jnp.dot(h, wgate_ref[:, sl], preferred_element_type=F32) + bgate_ref[:, sl])
            term = gate * jnp.dot(o, w_ref[:, c * mc:(c + 1) * mc], preferred_element_type=F32)
            merged = term if merged is None else merged + term
        part = jnp.dot(merged.astype(BF16), wout_ref[c * mc:(c + 1) * mc, :], preferred_element_type=F32)
        if c == 0:
            acc_ref[...] = part
        else:
            acc_ref[...] += part
    o_ref[...] = x + acc_ref[...]


def _post(x, oa, ob, oc, p, tm):
    n, d = x.shape
    row = lambda w: pl.BlockSpec((tm, w), lambda i: (i, 0))
    return pl.pallas_call(
        functools.partial(_post_body, mc=256),
        grid=(n // tm,),
        in_specs=[row(d), row(HW), row(HW), row(CW), _const_spec((1, d)),
                  _const_spec((d, 3 * d)), _const_spec((1, 3 * d)),
                  _const_spec((HW, d)), _const_spec((HW, d)), _const_spec((CW, d)), _const_spec((d, d))],
        out_specs=row(d),
        out_shape=jax.ShapeDtypeStruct((n, d), F32),
        scratch_shapes=[pltpu.VMEM((tm, d), F32)],
        compiler_params=_cparams("arbitrary"),
        name="post",
    )(x, oa, ob, oc, p["norm_mix"], p["w_gate"], p["b_gate"], p["w_o_a"], p["w_o_b"], p["w_o_c"], p["w_out"])


def _expand_rows(x, reps):
    return jnp.concatenate([jnp.broadcast_to(x[g:g + 1], (SUB, x.shape[1])) for g in range(reps)], axis=0)


def _new_visible():
    r = lax.broadcasted_iota(jnp.int32, (SUB, PAGE_SIZE), 0)
    c = lax.broadcasted_iota(jnp.int32, (SUB, PAGE_SIZE), 1)
    return c <= r


def _sscore_body(pt_ref, iq_ref, iw_ref, *refs, npg):
    pages, new_ref, keys_ref = refs[:npg], refs[npg], refs[npg + 1]

    def scores(kt):
        sc = jnp.maximum(jnp.dot(iq_ref[0], kt, preferred_element_type=F32), 0.0) * iw_ref[0]
        return sc[0:SUB] + sc[SUB:2 * SUB] + sc[2 * SUB:3 * SUB] + sc[3 * SUB:4 * SUB]

    for j in range(npg):
        keys_ref[0, :, j * PAGE_SIZE:(j + 1) * PAGE_SIZE] = _sort_key(scores(pages[j][...].astype(BF16)))
    s_new = jnp.where(_new_visible(), scores(new_ref[0]), -jnp.inf)
    keys_ref[0, :, npg * PAGE_SIZE:(npg + 1) * PAGE_SIZE] = _sort_key(s_new)
    pad = keys_ref.shape[2] - (npg + 1) * PAGE_SIZE
    if pad:
        keys_ref[0, :, (npg + 1) * PAGE_SIZE:] = jnp.full((SUB, pad), KEY_NEG_INF, jnp.int32)


def _sample_scores(pt, iq_rows, iw_rows, idx_t, new_idx_t, layer, ncols):
    nb = iq_rows.shape[0]
    npg = pt.shape[0] // nb
    page = lambda j: pl.BlockSpec((None, None, HEAD_DIM, PAGE_SIZE),
                                  lambda b, pt: (layer, pt[b * npg + j], 0, 0))
    return pl.pallas_call(
        functools.partial(_sscore_body, npg=npg),
        grid_spec=pltpu.PrefetchScalarGridSpec(
            num_scalar_prefetch=1,
            grid=(nb,),
            in_specs=[pl.BlockSpec((1, N_HEADS * SUB, HEAD_DIM), lambda b, pt: (b, 0, 0)),
                      pl.BlockSpec((1, N_HEADS * SUB, 1), lambda b, pt: (b, 0, 0))]
                     + [page(j) for j in range(npg)]
                     + [pl.BlockSpec((1, HEAD_DIM, PAGE_SIZE), lambda b, pt: (b, 0, 0))],
            out_specs=pl.BlockSpec((1, SUB, ncols), lambda b, pt: (b, 0, 0))),
        out_shape=jax.ShapeDtypeStruct((nb, SUB, ncols), jnp.int32),
        compiler_params=_cparams("arbitrary"),
        name="sample_scores",
    )(pt, iq_rows, iw_rows, *([idx_t] * npg), new_idx_t)


def _sselect_body(keys_ref, bias_ref, *, topk, pos_bits):
    nkeys, nq = keys_ref.shape
    thr, xpos = _topk_select(keys_ref, nkeys // COUNT_W, topk, pos_bits)
    pos = lax.broadcasted_iota(jnp.int32, (128, nq), 0)
    for g in range(nkeys // 128):
        sl = slice(g * 128, (g + 1) * 128)
        bias_ref[sl, :] = _select_bias(keys_ref[sl, :], pos + g * 128, thr, xpos)


def _sample_select(keys_t, topk):
    nkeys, nq = keys_t.shape
    tl = min(256, nq)
    return pl.pallas_call(
        functools.partial(_sselect_body, topk=topk, pos_bits=max(nkeys - 1, 1).bit_length()),
        grid=(nq // tl,),
        in_specs=[pl.BlockSpec((nkeys, tl), lambda i: (0, i))],
        out_specs=pl.BlockSpec((nkeys, tl), lambda i: (0, i)),
        out_shape=jax.ShapeDtypeStruct((nkeys, nq), F32),
        compiler_params=_cparams("arbitrary"),
        name="sample_select",
    )(keys_t)


def _sattn_body(pt_ref, aq_ref, bq_ref, cq_ref, bias_ref, biasn_ref, *refs, pg, n_steps, lam_init):
    it = iter(refs)
    take = lambda n: [next(it) for _ in range(n)]
    ak_refs, av_refs, lf_refs, bk_refs, bv_refs, ck_refs, cv_refs = (take(pg) for _ in range(7))
    akn_ref, avn_ref, lfn_ref, bkn_ref, bvn_ref, ckn_ref, cvn_ref = take(7)
    lq1_ref, lk1_ref, lq2_ref, lk2_ref, sg_ref = take(5)
    oa_ref, ob_ref, oc_ref = take(3)
    ma_ref, la_ref, acca_ref, mb_ref, lb_ref, accb_ref, mc_ref, lc_ref, accc_ref, ctot_ref = take(10)
    p = pl.program_id(1)
    nmap = 2 * N_HEADS

    @pl.when(p == 0)
    def _():
        for m_ref, l_ref, acc_ref in ((ma_ref, la_ref, acca_ref), (mb_ref, lb_ref, accb_ref),
                                      (mc_ref, lc_ref, accc_ref)):
            m_ref[...] = jnp.full(m_ref.shape, NEG, F32)
            l_ref[...] = jnp.zeros(l_ref.shape, F32)
            acc_ref[...] = jnp.zeros(acc_ref.shape, F32)
        ctot_ref[...] = jnp.zeros(ctot_ref.shape, F32)

    r = lax.broadcasted_iota(jnp.int32, (PAGE_SIZE, PAGE_SIZE), 0)
    c = lax.broadcasted_iota(jnp.int32, (PAGE_SIZE, PAGE_SIZE), 1)
    tri = jnp.where(r <= c, 1.0, 0.0).astype(F32)
    cat = lambda xs, axis: jnp.concatenate(xs, axis=axis) if len(xs) > 1 else xs[0]

    def step(akt, avt, lfs, bkt, bvt, ckt, cvh, bias, vis):
        n = len(akt)
        pad = jnp.zeros((SUB - N_HEADS, PAGE_SIZE), F32)
        in_page = jnp.dot(cat([x for lf in lfs for x in (lf, pad)], 0), tri, preferred_element_type=F32,
                          precision=lax.Precision.HIGHEST)
        tot = ctot_ref[...]
        cums = []
        for j in range(n):
            cum = in_page[j * SUB:(j + 1) * SUB] + tot
            tot = cum[:, PAGE_SIZE - 1:PAGE_SIZE]
            cums.append(_expand_rows(cum, N_HEADS))
        ctot_ref[...] = tot
        s = jnp.dot(aq_ref[0], cat(akt, 1), preferred_element_type=F32) - cat(cums, 1)
        if vis is not None:
            s = jnp.where(jnp.concatenate([vis] * N_HEADS, axis=0), s, NEG)
        pa, m_new, l_new, alpha = _online_update(s, ma_ref[...], la_ref[...])
        ma_ref[...] = m_new
        la_ref[...] = l_new
        acca_ref[...] = alpha * acca_ref[...] + _qk(pa.astype(BF16), cat(avt, 1))
        s = jnp.dot(bq_ref[0], cat(bkt, 1), preferred_element_type=F32) + jnp.concatenate([bias] * N_HEADS, axis=0)
        pb, m_new, l_new, alpha = _online_update(s, mb_ref[...], lb_ref[...])
        mb_ref[...] = m_new
        lb_ref[...] = l_new
        accb_ref[...] = alpha * accb_ref[...] + _qk(pb.astype(BF16), cat(bvt, 1))
        s = jnp.dot(cq_ref[0], cat(ckt, 1), preferred_element_type=F32)
        if vis is not None:
            s = jnp.where(jnp.concatenate([vis] * nmap, axis=0), s, NEG)
        pc, m_new, l_new, alpha = _online_update(s, mc_ref[...], lc_ref[...])
        mc_ref[...] = m_new
        lc_ref[...] = l_new
        pc = pc.astype(BF16)
        for head in range(N_HEADS):
            rows = slice(head * 2 * SUB, (head + 1) * 2 * SUB)
            pv = jnp.dot(pc[rows], cat([cvh[j][head] for j in range(n)], 0), preferred_element_type=F32)
            accc_ref[rows, :] = alpha[rows] * accc_ref[rows, :] + pv

    bf = lambda refs_: [rf[...].astype(BF16) for rf in refs_]
    step(bf(ak_refs), bf(av_refs), [rf[...] for rf in lf_refs], bf(bk_refs), bf(bv_refs), bf(ck_refs),
         [[rf[pl.ds(head, PAGE_SIZE, stride=N_HEADS), :].astype(BF16) for head in range(N_HEADS)]
          for rf in cv_refs], bias_ref[0], None)

    @pl.when(p == n_steps - 1)
    def _():
        cvn = cvn_ref[0]
        step([akn_ref[0]], [avn_ref[0]], [lfn_ref[0]], [bkn_ref[0]], [bvn_ref[0]], [ckn_ref[0]],
             [[cvn[:, head * 2 * HEAD_DIM:(head + 1) * 2 * HEAD_DIM] for head in range(N_HEADS)]],
             biasn_ref[0], _new_visible())
        lho = _lane_head((SUB, HW))
        for acc_ref, l_ref, o_ref in ((acca_ref, la_ref, oa_ref), (accb_ref, lb_ref, ob_ref)):
            o = acc_ref[...] / l_ref[...]
            out = jnp.zeros((SUB, HW), F32)
            for h in range(N_HEADS):
                out = jnp.where(lho == h, o[h * SUB:(h + 1) * SUB], out)
            o_ref[0] = out.astype(BF16)
        lam = _lambda(lq1_ref[...], lk1_ref[...], lq2_ref[...], lk2_ref[...], lam_init)
        accc = accc_ref[...]
        lc = lc_ref[...]
        for head in range(N_HEADS):
            r1 = slice(head * 2 * SUB, head * 2 * SUB + SUB)
            r2 = slice(head * 2 * SUB + SUB, (head + 1) * 2 * SUB)
            o = _diff_finish(accc[r1], lc[r1], accc[r2], lc[r2], lam, sg_ref[...], 1.0 - lam_init)
            oc_ref[0, :, head * 2 * HEAD_DIM:(head + 1) * 2 * HEAD_DIM] = o.astype(BF16)


def _sample_attention(pt, q_rows, bias, caches, new_rows, p, layer, lam_init, pg):
    aq, bq, cq = q_rows
    nb = aq.shape[0]
    npg = pt.shape[0] // nb
    n_steps = npg // pg
    nmap = 2 * N_HEADS

    def pages(shape):
        nd = len(shape)
        return [pl.BlockSpec((None, None) + shape,
                             lambda b, p, pt, j=j: (layer, pt[b * npg + p * pg + j]) + (0,) * nd)
                for j in range(pg)]

    per_seq = lambda shape: pl.BlockSpec((1,) + shape, lambda b, p, pt: (b,) + (0,) * len(shape))
    const = lambda shape: pl.BlockSpec(shape, lambda b, p, pt: (0,) * len(shape))
    in_specs = ([per_seq((N_HEADS * SUB, HW)), per_seq((N_HEADS * SUB, HW)), per_seq((nmap * SUB, CW)),
                 pl.BlockSpec((1, SUB, pg * PAGE_SIZE), lambda b, p, pt: (b, 0, p)),
                 pl.BlockSpec((1, SUB, PAGE_SIZE), lambda b, p, pt: (b, 0, npg))]
                + pages((HW, PAGE_SIZE)) + pages((HW, PAGE_SIZE)) + pages((N_HEADS, PAGE_SIZE))
                + pages((HW, PAGE_SIZE)) + pages((HW, PAGE_SIZE)) + pages((CW, PAGE_SIZE))
                + pages((PAGE_SIZE * N_HEADS, 2 * HEAD_DIM))
                + [per_seq((HW, PAGE_SIZE)), per_seq((HW, PAGE_SIZE)), per_seq((N_HEADS, PAGE_SIZE)),
                   per_seq((HW, PAGE_SIZE)), per_seq((HW, PAGE_SIZE)), per_seq((CW, PAGE_SIZE)),
                   per_seq((PAGE_SIZE, CW)),
                   const((1, HEAD_DIM)), const((1, HEAD_DIM)), const((1, HEAD_DIM)), const((1, HEAD_DIM)),
                   const((1, 2 * HEAD_DIM))])
    cache_args = [c for c in caches for _ in range(pg)]
    out_specs = [per_seq((SUB, HW)), per_seq((SUB, HW)), per_seq((SUB, CW))]
    scratch = [pltpu.VMEM((N_HEADS * SUB, 1), F32), pltpu.VMEM((N_HEADS * SUB, 1), F32),
               pltpu.VMEM((N_HEADS * SUB, HW), F32),
               pltpu.VMEM((N_HEADS * SUB, 1), F32), pltpu.VMEM((N_HEADS * SUB, 1), F32),
               pltpu.VMEM((N_HEADS * SUB, HW), F32),
               pltpu.VMEM((nmap * SUB, 1), F32), pltpu.VMEM((nmap * SUB, 1), F32),
               pltpu.VMEM((nmap * SUB, 2 * HEAD_DIM), F32),
               pltpu.VMEM((SUB, 1), F32)]
    return pl.pallas_call(
        functools.partial(_sattn_body, pg=pg, n_steps=n_steps, lam_init=lam_init),
        grid_spec=pltpu.PrefetchScalarGridSpec(
            num_scalar_prefetch=1, grid=(nb, n_steps),
            in_specs=in_specs, out_specs=out_specs, scratch_shapes=scratch),
        out_shape=[jax.ShapeDtypeStruct((nb, SUB, HW), BF16), jax.ShapeDtypeStruct((nb, SUB, HW), BF16),
                   jax.ShapeDtypeStruct((nb, SUB, CW), BF16)],
        compiler_params=_cparams("arbitrary", "arbitrary"),
        name="sample_attention",
    )(pt, aq, bq, cq, bias, bias, *cache_args, *new_rows, p["lq1"], p["lk1"], p["lq2"], p["lk2"], p["c_subln"])


def _rope_tables(pos):
    half = ROPE_DIM // 2
    inv_freq = ROPE_THETA ** (-jnp.arange(half, dtype=F32) / half)
    ang = pos.astype(F32)[:, None] * inv_freq[None, :]
    cos, sin = jnp.cos(ang), jnp.sin(ang)
    n = pos.shape[0]
    one = jnp.ones((n, HEAD_DIM - ROPE_DIM), F32)
    zero = jnp.zeros((n, HEAD_DIM - ROPE_DIM), F32)
    z8 = jnp.zeros((n, half), F32)
    c = jnp.concatenate([cos, cos, one], axis=1)
    sa = jnp.concatenate([z8, sin, zero], axis=1)
    sb = jnp.concatenate([-sin, z8, zero], axis=1)
    return tuple(jnp.tile(t, (1, 2)) for t in (c, sa, sb))


def _masked_query_rows(q, n_seq, n_new, n_groups):
    w = q.shape[1]
    q = q.reshape(n_seq, 1, n_new, w)
    q = jnp.pad(q, ((0, 0), (0, 0), (0, SUB - n_new), (0, 0)))
    grp = (jnp.arange(w) // HEAD_DIM)[None, :] == jnp.arange(n_groups)[:, None]
    q = jnp.where(grp[None, :, None, :], q, jnp.zeros((), q.dtype))
    return q.reshape(n_seq, n_groups * SUB, w)


def _pad_keys(x, n_seq, n_new):
    x = x.reshape(n_seq, n_new, x.shape[1])
    return jnp.pad(x, ((0, 0), (0, PAGE_SIZE - n_new), (0, 0)))


def _pad_keys_t(x, n_seq, n_new):
    x = jnp.swapaxes(x.reshape(n_seq, n_new, x.shape[1]), 1, 2)
    return jnp.pad(x, ((0, 0), (0, 0), (0, PAGE_SIZE - n_new)))


def _layer_params(l, a):
    bf = lambda x: x.astype(BF16)
    row = lambda x: x.reshape(1, -1)
    d = a["norm_mix"].shape[1]
    fox_bias = jnp.zeros((1, MISC_W), F32).at[0, :N_HEADS].set(a["fox_bias_f"][l])
    blk = jnp.arange(HW) // HEAD_DIM
    bd = jnp.where(blk[:, None] == blk[None, :], 1.0 / HEAD_DIM, 0.0).astype(BF16)
    return {
        "norm_ffn1": row(a["norm_ffn1"][l]), "norm_mix": row(a["norm_mix"][l]), "norm_ffn2": row(a["norm_ffn2"][l]),
        "ffn1": (bf(a["ffn1_w_gate"][l]), bf(a["ffn1_w_up"][l]), bf(a["ffn1_w_down"][l])),
        "ffn2": (bf(a["ffn2_w_gate"][l]), bf(a["ffn2_w_up"][l]), bf(a["ffn2_w_down"][l])),
        "w_in": bf(_pack_w_in(a["w_in"][l])), "fox_bias": fox_bias, "bd": bd,
        "gaq": row(jnp.tile(a["a_q_norm"][l], N_HEADS)), "gak": row(jnp.tile(a["a_k_norm"][l], N_HEADS)),
        "gbq": row(jnp.tile(a["b_q_norm"][l], N_HEADS)), "gbk": row(jnp.tile(a["b_k_norm"][l], N_HEADS)),
        "gcq": row(jnp.tile(a["c_q_norm"][l].reshape(-1), N_HEADS)),
        "gck": row(jnp.tile(a["c_k_norm"][l].reshape(-1), N_HEADS)),
        "lq1": row(a["c_lambda_q1"][l]), "lk1": row(a["c_lambda_k1"][l]),
        "lq2": row(a["c_lambda_q2"][l]), "lk2": row(a["c_lambda_k2"][l]),
        "c_subln": row(a["c_subln"][l]),
        "w_gate": bf(a["w_gate"][l]), "b_gate": row(a["b_gate"][l]),
        "w_o_a": bf(a["w_o_a"][l]), "w_o_b": bf(a["w_o_b"][l]), "w_o_c": bf(a["w_o_c"][l]),
        "w_out": bf(a["w_out"][l]),
    }


_INPROJ_NAMES = ("aq", "akf", "akb", "avf", "avb", "bq", "bkf", "bkb", "bvf", "bvb", "biq", "bif", "bib",
                 "cq", "ckf", "ckb", "cvf", "cvb", "misc", "cum")


def kernel(x_prompt, x_sample, cache_a_k, cache_a_v, cache_a_logf, cache_b_k, cache_b_v, cache_b_idx, cache_c_k, cache_c_v, page_table, norm_ffn1, ffn1_w_gate, ffn1_w_up, ffn1_w_down, norm_mix, w_in, fox_bias_f, a_q_norm, a_k_norm, b_q_norm, b_k_norm, c_q_norm, c_k_norm, c_lambda_q1, c_lambda_k1, c_lambda_q2, c_lambda_k2, c_subln, w_o_a, w_o_b, w_o_c, w_gate, b_gate, w_out, norm_ffn2, ffn2_w_gate, ffn2_w_up, ffn2_w_down):
    a = dict(norm_ffn1=norm_ffn1, ffn1_w_gate=ffn1_w_gate, ffn1_w_up=ffn1_w_up, ffn1_w_down=ffn1_w_down,
             norm_mix=norm_mix, w_in=w_in, fox_bias_f=fox_bias_f, a_q_norm=a_q_norm, a_k_norm=a_k_norm,
             b_q_norm=b_q_norm, b_k_norm=b_k_norm, c_q_norm=c_q_norm, c_k_norm=c_k_norm,
             c_lambda_q1=c_lambda_q1, c_lambda_k1=c_lambda_k1, c_lambda_q2=c_lambda_q2, c_lambda_k2=c_lambda_k2,
             c_subln=c_subln, w_o_a=w_o_a, w_o_b=w_o_b, w_o_c=w_o_c, w_gate=w_gate, b_gate=b_gate, w_out=w_out,
             norm_ffn2=norm_ffn2, ffn2_w_gate=ffn2_w_gate, ffn2_w_up=ffn2_w_up, ffn2_w_down=ffn2_w_down)
    depth = w_in.shape[0]
    bsz, seq, d = x_prompt.shape
    nb, n_new, _ = x_sample.shape
    n_pool = cache_a_k.shape[1]
    npg = page_table.shape[1]
    past_len = npg * PAGE_SIZE
    np_tok, ns_tok = bsz * seq, nb * n_new
    tm_p = min(512, seq)
    tm_s = min(512, ns_tok)
    tq = min(256, seq)
    tk = min(512, seq)

    assert n_new <= SUB and np_tok % tm_p == 0 and ns_tok % tm_s == 0 and tm_s % n_new == 0
    keys_last = lambda c, w: jnp.moveaxis(c, 2, -1).reshape(depth, n_pool, w, PAGE_SIZE)
    caches = (keys_last(cache_a_k, HW), keys_last(cache_a_v, HW), keys_last(cache_a_logf, N_HEADS),
              keys_last(cache_b_k, HW), keys_last(cache_b_v, HW), keys_last(cache_c_k, CW),
              cache_c_v.reshape(depth, n_pool, PAGE_SIZE * N_HEADS, 2 * HEAD_DIM))
    idx_t = keys_last(cache_b_idx, HEAD_DIM)
    pt = page_table.reshape(-1).astype(jnp.int32)
    pg = next(g for g in (8, 4, 2, 1) if npg % g == 0)
    tk_total = past_len + n_new
    ncols = -(-(npg + 1) * PAGE_SIZE // COUNT_W) * COUNT_W
    topk_s = min(TOPK_MAX, tk_total // 4)

    tabs_p = _rope_tables(jnp.arange(seq, dtype=jnp.int32))
    pos_s = past_len + jnp.arange(n_new, dtype=jnp.int32)
    tabs_s = _rope_tables(jnp.tile(pos_s, tm_s // n_new))

    xp = x_prompt.reshape(np_tok, d)
    xs = x_sample.reshape(ns_tok, d)
    rows_p, rows_s = [], []
    for l in range(depth):
        p = _layer_params(l, a)
        lam_init = 0.8 - 0.6 * math.exp(-0.3 * l)

        xp = _ffn(xp, p["norm_ffn1"], *p["ffn1"], tm_p)
        pr = dict(zip(_INPROJ_NAMES, _inproj(xp, p, tabs_p, tm_p, seq)))
        b3 = lambda z: z.reshape(bsz, seq, z.shape[1])
        b3t = lambda z: jnp.swapaxes(b3(z), 1, 2)
        kcum = jnp.swapaxes(b3(pr["cum"])[:, :, :SUB], 1, 2)
        oa = _fox_prompt(b3(pr["aq"]), b3(pr["akb"]), b3(pr["avb"]), kcum, tq, tk)
        iw_t = jnp.swapaxes(b3(pr["misc"])[:, :, N_HEADS:N_HEADS + SUB], 1, 2)
        ob = _dsa_prompt(b3(pr["bq"]), b3(pr["biq"]), iw_t, b3(pr["bkb"]), b3(pr["bvb"]), b3(pr["bib"]), tq, tk)
        oc = _diff_prompt(b3(pr["cq"]), b3(pr["ckb"]), b3t(pr["cvb"]), p, lam_init, tq, tk)
        xp = _post(xp, oa.reshape(np_tok, HW), ob.reshape(np_tok, HW), oc.reshape(np_tok, CW), p, tm_p)
        xp = _ffn(xp, p["norm_ffn2"], *p["ffn2"], tm_p)
        rows_p.append(pr)

        xs = _ffn(xs, p["norm_ffn1"], *p["ffn1"], tm_s)
        sr = dict(zip(_INPROJ_NAMES, _inproj(xs, p, tabs_s, tm_s, ns_tok)))
        iq = jnp.swapaxes(sr["biq"].reshape(nb, n_new, N_HEADS, HEAD_DIM), 1, 2)
        iq_rows = jnp.pad(iq, ((0, 0), (0, 0), (0, SUB - n_new), (0, 0))).reshape(nb, N_HEADS * SUB, HEAD_DIM)
        iw = sr["misc"][:, N_HEADS:2 * N_HEADS].reshape(nb, n_new, N_HEADS)
        iw = jnp.pad(jnp.swapaxes(iw, 1, 2), ((0, 0), (0, 0), (0, SUB - n_new)))
        iw_rows = iw.reshape(nb, N_HEADS * SUB, 1)
        new_idx_t = _pad_keys_t(sr["bib"][:, :HEAD_DIM], nb, n_new)
        keys = _sample_scores(pt, iq_rows, iw_rows, idx_t, new_idx_t, l, ncols)
        bias = _sample_select(keys.reshape(nb * SUB, ncols).T, topk_s).T.reshape(nb, SUB, ncols)
        q_rows = (_masked_query_rows(sr["aq"], nb, n_new, N_HEADS),
                  _masked_query_rows(sr["bq"], nb, n_new, N_HEADS),
                  _masked_query_rows(sr["cq"], nb, n_new, 2 * N_HEADS))
        new_rows = (_pad_keys_t(sr["akb"], nb, n_new), _pad_keys_t(sr["avb"], nb, n_new),
                    _pad_keys_t(sr["misc"][:, :N_HEADS], nb, n_new),
                    _pad_keys_t(sr["bkb"], nb, n_new), _pad_keys_t(sr["bvb"], nb, n_new),
                    _pad_keys_t(sr["ckb"], nb, n_new), _pad_keys(sr["cvb"], nb, n_new))
        soa, sob, soc = _sample_attention(pt, q_rows, bias, caches, new_rows, p, l, lam_init, pg)
        take = lambda o: o[:, :n_new].reshape(ns_tok, o.shape[2])
        xs = _post(xs, take(soa), take(sob), take(soc), p, tm_s)
        xs = _ffn(xs, p["norm_ffn2"], *p["ffn2"], tm_s)
        rows_s.append(sr)

    def stack(rows, name, lead, tail):
        z = jnp.stack([r[name] for r in rows])
        if name == "misc":
            z = z[:, :, :N_HEADS]
        elif name != "cvf":
            z = jnp.swapaxes(z, 2, 3)
        return z.reshape((depth,) + lead + tail)

    lead_p, lead_s = (bsz, seq), (nb, n_new)
    hd = (N_HEADS, HEAD_DIM)
    out = [xp.reshape(bsz, seq, d), xs.reshape(nb, n_new, d)]
    for name, tail in (("akf", hd), ("avf", hd), ("misc", (N_HEADS,)), ("bkf", hd), ("bvf", hd),
                       ("bif", (HEAD_DIM,)), ("ckf", (N_HEADS, 2, HEAD_DIM)), ("cvf", (N_HEADS, 2 * HEAD_DIM))):
        out.append(stack(rows_p, name, lead_p, tail))
        out.append(stack(rows_s, name, lead_s, tail))
    return tuple(out)
```
